```python
import jax, jax.numpy as jnp
from jax import lax
import numpy as np

D_MODEL = 1024
BATCH = 16
SEQ = 2048
DEPTH = 4

CONV_WIDTH = D_MODEL // 2
CONV_GROUPS = 8
CONV_K = 3
HEAD_DIM = 64
ATT_WIDTH = D_MODEL - CONV_WIDTH
ATT_HEADS = ATT_WIDTH // HEAD_DIM
IN_WIDTH = 3 * CONV_WIDTH + 3 * ATT_WIDTH
MOBA_BLOCK = 256
MOBA_TOPK = 3
QUERY_CHUNK = 32
ROPE_THETA = 10000.0
N_GROUPS = 4
EXPERTS_PER_GROUP = 4
N_EXPERTS = N_GROUPS * EXPERTS_PER_GROUP
TOP_K_IN_GROUP = 2
D_FF_EXPERT = D_MODEL // 4
EPS = 1e-6
NEG = -1e30

kernel_name = "hymba_conv_moba_hmoe_adaln"


def rmsnorm(x, g):
    xf = x.astype(jnp.float32)
    y = xf * lax.rsqrt(jnp.mean(xf * xf, axis=-1, keepdims=True) + EPS)
    return (y * g.astype(jnp.float32)).astype(x.dtype)


def rope(x, positions):
    dh = x.shape[-1]
    inv_freq = ROPE_THETA ** (-jnp.arange(0, dh, 2, dtype=jnp.float32) / dh)
    ang = positions[:, None, :, None].astype(jnp.float32) * inv_freq
    cos, sin = jnp.cos(ang), jnp.sin(ang)
    xf = x.astype(jnp.float32)
    x1, x2 = xf[..., : dh // 2], xf[..., dh // 2:]
    out = jnp.concatenate([x1 * cos - x2 * sin, x2 * cos + x1 * sin], axis=-1)
    return out.astype(x.dtype)


def short_conv_mixer(u_b, u_c, u_h, conv_w):
    gated = u_c * u_h
    rhs = conv_w.reshape(CONV_K, 1, CONV_WIDTH).astype(gated.dtype)
    conv = lax.conv_general_dilated(
        gated, rhs, window_strides=(1,), padding=[(CONV_K - 1, 0)],
        dimension_numbers=("NWC", "WIO", "NWC"), feature_group_count=CONV_WIDTH)
    return u_b * conv


def moba_attention(q, k, v):
    B, H, S, dh = q.shape
    nb = -(-S // MOBA_BLOCK)
    s_pad = nb * MOBA_BLOCK
    pad = [(0, 0), (0, 0), (0, s_pad - S), (0, 0)]
    kb = jnp.pad(k, pad).reshape(B, H, nb, MOBA_BLOCK, dh)
    vb = jnp.pad(v, pad).reshape(B, H, nb, MOBA_BLOCK, dh)
    kmean = jnp.mean(kb.astype(jnp.float32), axis=3)
    n_sel = max(1, min(MOBA_TOPK, nb - 1))
    scale = HEAD_DIM ** -0.5
    n_chunks = S // QUERY_CHUNK
    gather = jax.vmap(jax.vmap(lambda blocks, ix: blocks[ix]))
    blk = jnp.arange(nb)

    def chunk(ci):
        start = ci * QUERY_CHUNK
        qc = lax.dynamic_slice_in_dim(q, start, QUERY_CHUNK, axis=2)
        own = start // MOBA_BLOCK
        qpos = start + jnp.arange(QUERY_CHUNK)
        kpos = own * MOBA_BLOCK + jnp.arange(MOBA_BLOCK)
        k_own = lax.dynamic_index_in_dim(kb, own, axis=2, keepdims=False)
        v_own = lax.dynamic_index_in_dim(vb, own, axis=2, keepdims=False)
        s_own = jnp.einsum("bhqd,bhkd->bhqk", qc, k_own,
                           preferred_element_type=jnp.float32) * scale
        s_own = jnp.where(kpos[None, :] <= qpos[:, None], s_own, NEG)
        gscore = jnp.einsum("bhqd,bhnd->bhqn", qc.astype(jnp.float32), kmean)
        gscore = jnp.where(blk < own, gscore, NEG)
        _, sel = lax.top_k(gscore, n_sel)
        valid = sel < own
        k_sel = gather(kb, sel)
        v_sel = gather(vb, sel)
        s_sel = jnp.einsum("bhqd,bhqnkd->bhqnk", qc, k_sel,
                           preferred_element_type=jnp.float32) * scale
        s_sel = jnp.where(valid[..., None], s_sel, NEG)
        n_l = n_sel * MOBA_BLOCK
        s_all = jnp.concatenate(
            [s_sel.reshape(B, H, QUERY_CHUNK, n_l), s_own], axis=-1)
        p = jax.nn.softmax(s_all, axis=-1)
        p_sel = p[..., :n_l].reshape(B, H, QUERY_CHUNK, n_sel, MOBA_BLOCK).astype(v.dtype)
        p_own = p[..., n_l:].astype(v.dtype)
        out = (jnp.einsum("bhqnk,bhqnkd->bhqd", p_sel, v_sel)
               + jnp.einsum("bhqk,bhkd->bhqd", p_own, v_own))
        return out.astype(q.dtype)

    outs = lax.map(chunk, jnp.arange(n_chunks))
    return outs.transpose(1, 2, 0, 3, 4).reshape(B, H, S, dh)


def hier_moe(h, w_rg, b_rg, w_re, b_re, w_gate, w_up, w_down):
    B, S, D = h.shape
    n = B * S
    t = h.reshape(n, D)
    tok = jnp.arange(n)
    g_logits = (t @ w_rg).astype(jnp.float32) + b_rg.astype(jnp.float32)
    g_prob = jax.nn.softmax(g_logits, axis=-1)
    g_sel = jnp.argmax(g_logits, axis=-1)
    p_group = g_prob[tok, g_sel]
    e_logits = ((t @ w_re).astype(jnp.float32) + b_re.astype(jnp.float32)
                ).reshape(n, N_GROUPS, EXPERTS_PER_GROUP)
    e_in = e_logits[tok, g_sel]
    top_v, top_i = lax.top_k(e_in, TOP_K_IN_GROUP)
    w_sel = jax.nn.softmax(top_v, axis=-1) * p_group[:, None]
    expert_ids = g_sel[:, None] * EXPERTS_PER_GROUP + top_i
    combine = jnp.sum(jax.nn.one_hot(expert_ids, N_EXPERTS, dtype=jnp.float32)
                      * w_sel[..., None], axis=1)
    hid = (jax.nn.silu(jnp.einsum("nd,edf->nef", t, w_gate))
           * jnp.einsum("nd,edf->nef", t, w_up))
    y = jnp.einsum("nef,efd->nd", hid * combine[:, :, None].astype(hid.dtype), w_down)
    return y.reshape(B, S, D)


def setup_inputs(seed: int = 0) -> dict:
    key = jax.random.key(seed)
    ks = jax.random.split(key, 24)
    f32 = jnp.float32
    D, L = D_MODEL, DEPTH

    def nrm(k, shape, s):
        return jax.random.normal(k, shape, f32) * s

    x = jax.random.normal(ks[0], (BATCH, SEQ, D), f32)
    c = jax.random.normal(ks[1], (BATCH, D), f32)
    offset = jax.random.randint(ks[2], (BATCH, 1), 0, 4096, dtype=jnp.int32)
    positions = offset + jnp.arange(SEQ, dtype=jnp.int32)[None, :]
    return {
        "x": x,
        "c": c,
        "positions": positions,
        "norm1_g": 1.0 + nrm(ks[3], (L, D), 0.02),
        "norm2_g": 1.0 + nrm(ks[4], (L, D), 0.02),
        "w_ada": nrm(ks[5], (L, D, 6 * D), 0.5 * D ** -0.5),
        "b_ada": nrm(ks[6], (L, 6 * D), 0.02),
        "w_in": nrm(ks[7], (L, D, IN_WIDTH), D ** -0.5),
        "conv_w": nrm(ks[8], (L, CONV_K, CONV_WIDTH), CONV_K ** -0.5),
        "q_norm_g": 1.0 + nrm(ks[9], (L, HEAD_DIM), 0.02),
        "k_norm_g": 1.0 + nrm(ks[10], (L, HEAD_DIM), 0.02),
        "conv_out_g": 1.0 + nrm(ks[11], (L, CONV_WIDTH), 0.02),
        "attn_out_g": 1.0 + nrm(ks[12], (L, ATT_WIDTH), 0.02),
        "w_out": nrm(ks[13], (L, D, D), D ** -0.5),
        "w_router_group": nrm(ks[14], (L, D, N_GROUPS), D ** -0.5),
        "b_router_group": nrm(ks[15], (L, N_GROUPS), 0.01),
        "w_router_expert": nrm(ks[16], (L, D, N_EXPERTS), D ** -0.5),
        "b_router_expert": nrm(ks[17], (L, N_EXPERTS), 0.01),
        "w_gate": nrm(ks[18], (L, N_EXPERTS, D, D_FF_EXPERT), D ** -0.5),
        "w_up": nrm(ks[19], (L, N_EXPERTS, D, D_FF_EXPERT), D ** -0.5),
        "w_down": nrm(ks[20], (L, N_EXPERTS, D_FF_EXPERT, D), D_FF_EXPERT ** -0.5),
    }


def reference(x, c, positions, norm1_g, norm2_g, w_ada, b_ada, w_in, conv_w,
              q_norm_g, k_norm_g, conv_out_g, attn_out_g, w_out,
              w_router_group, b_router_group, w_router_expert, b_router_expert,
              w_gate, w_up, w_down):
    B, S, D = x.shape
    cw, aw = CONV_WIDTH, ATT_WIDTH
    split_at = [cw, 2 * cw, 3 * cw, 3 * cw + aw, 3 * cw + 2 * aw]
    c_act = jax.nn.silu(c)

    def heads(u):
        return u.reshape(B, S, ATT_HEADS, HEAD_DIM).transpose(0, 2, 1, 3)

    for l in range(DEPTH):
        mod = (c_act @ w_ada[l] + b_ada[l])[:, None, :]
        shift1, scale1, gate1, shift2, scale2, gate2 = jnp.split(mod, 6, axis=-1)

        h = rmsnorm(x, norm1_g[l]) * (1.0 + scale1) + shift1
        u = h @ w_in[l]
        u_b, u_c, u_h, u_q, u_k, u_v = jnp.split(u, split_at, axis=-1)
        y_conv = short_conv_mixer(u_b, u_c, u_h, conv_w[l])
        q = rope(rmsnorm(heads(u_q), q_norm_g[l]), positions)
        k = rope(rmsnorm(heads(u_k), k_norm_g[l]), positions)
        v = heads(u_v)
        y_att = moba_attention(q, k, v).transpose(0, 2, 1, 3).reshape(B, S, aw)
        y_mix = jnp.concatenate(
            [rmsnorm(y_conv, conv_out_g[l]), rmsnorm(y_att, attn_out_g[l])], axis=-1)
        x = x + gate1 * (y_mix @ w_out[l])

        h2 = rmsnorm(x, norm2_g[l]) * (1.0 + scale2) + shift2
        x = x + gate2 * hier_moe(h2, w_router_group[l], b_router_group[l],
                                 w_router_expert[l], b_router_expert[l],
                                 w_gate[l], w_up[l], w_down[l])
    return x
```

```python
import functools

import jax
import jax.numpy as jnp
from jax import lax
from jax.experimental import pallas as pl
from jax.experimental.pallas import tpu as pltpu

F32 = jnp.float32
BF16 = jnp.bfloat16

D_MODEL = 1024
CONV_WIDTH = 512
CONV_K = 3
HEAD_DIM = 64
HALF_DIM = HEAD_DIM // 2
ATT_WIDTH = 512
ATT_HEADS = ATT_WIDTH // HEAD_DIM
IN_WIDTH = 3 * CONV_WIDTH + 3 * ATT_WIDTH
MOBA_BLOCK = 256
MOBA_TOPK = 3
ROPE_THETA = 10000.0
N_GROUPS = 4
EXPERTS_PER_GROUP = 4
N_EXPERTS = N_GROUPS * EXPERTS_PER_GROUP
D_FF_EXPERT = D_MODEL // 4
EPS = 1e-6
NEG = -1e30
LOWEST = -3e38

LANES = 128
HEAD_PAIR = 2 * HEAD_DIM
N_PAIRS = ATT_WIDTH // HEAD_PAIR
TOKEN_TILE = MOBA_BLOCK
PAIRS_PER_GROUP = 6
N_BUCKETS = N_GROUPS * PAIRS_PER_GROUP
EXPERT_TILE = 256
ROW_WORDS = D_MODEL + LANES
MOVE_TILE = 512
VMEM_LIMIT = 48 * 1024 * 1024


def _params(n_axes, vmem=VMEM_LIMIT):
    return pltpu.CompilerParams(dimension_semantics=("arbitrary",) * n_axes, vmem_limit_bytes=vmem)


def _rmsnorm(x, g):
    return x * lax.rsqrt(jnp.mean(x * x, axis=-1, keepdims=True) + EPS) * g


def _silu(x):
    return x * (1.0 / (1.0 + jnp.exp(-x)))


def _mod_kernel(c_ref, w_ref, b_ref, o_ref):
    ca = _silu(c_ref[...]).astype(BF16)
    o_ref[0] = jnp.dot(ca, w_ref[0].astype(BF16), preferred_element_type=F32) + b_ref[0]


def _modulation(c, w_ada, b_ada):
    depth, d, width = w_ada.shape
    bsz = c.shape[0]
    tn = 1536
    return pl.pallas_call(
        _mod_kernel,
        grid=(depth, width // tn),
        in_specs=[
            pl.BlockSpec((bsz, d), lambda l, j: (0, 0)),
            pl.BlockSpec((1, d, tn), lambda l, j: (l, 0, j)),
            pl.BlockSpec((1, 1, tn), lambda l, j: (l, 0, j)),
        ],
        out_specs=pl.BlockSpec((1, bsz, tn), lambda l, j: (l, 0, j)),
        out_shape=jax.ShapeDtypeStruct((depth, bsz, width), F32),
        compiler_params=_params(2),
        name="modulation",
    )(c, w_ada, b_ada.reshape(depth, 1, width))


def _rope_kernel(pos_ref, freq_ref, cos_ref, sin_ref):
    ts = pos_ref.shape[-1]
    freq = jnp.concatenate([freq_ref[...]] * (ts // LANES), axis=1)
    ang = pos_ref[0] * freq
    cos_ref[0] = jnp.cos(ang)
    sin_ref[0] = jnp.sin(ang)


def _rope_tables(positions):
    bsz, seq = positions.shape
    ts = 512
    inv_freq = ROPE_THETA ** (-jnp.arange(0, HEAD_DIM, 2, dtype=F32) / HEAD_DIM)
    freq = jnp.broadcast_to(inv_freq[:, None], (HALF_DIM, LANES))
    pos = positions.astype(F32).reshape(bsz, 1, seq)
    spec = pl.BlockSpec((1, HALF_DIM, ts), lambda b, t: (b, 0, t))
    return pl.pallas_call(
        _rope_kernel,
        grid=(bsz, seq // ts),
        in_specs=[pl.BlockSpec((1, 1, ts), lambda b, t: (b, 0, t)),
                  pl.BlockSpec((HALF_DIM, LANES), lambda b, t: (0, 0))],
        out_specs=[spec, spec],
        out_shape=[jax.ShapeDtypeStruct((bsz, HALF_DIM, seq), F32)] * 2,
        compiler_params=_params(2),
        name="rope_tables",
    )(pos, freq)


def _headnorm_rope_t(z_t, gain, cos_t, sin_t):
    outs = []
    for hd in range(ATT_HEADS):
        z = z_t[hd * HEAD_DIM:(hd + 1) * HEAD_DIM]
        zn = z * lax.rsqrt(jnp.mean(z * z, axis=0, keepdims=True) + EPS) * gain
        x1, x2 = zn[:HALF_DIM], zn[HALF_DIM:]
        outs.append(x1 * cos_t - x2 * sin_t)
        outs.append(x2 * cos_t + x1 * sin_t)
    return jnp.concatenate(outs, axis=0)


def _mixer_in_kernel(x_ref, mod_ref, g1_ref, win_ref, cw_ref, cg_ref, qg_ref, kg_ref, cos_ref, sin_ref,
                     yc_ref, qt_ref, k_ref, vt_ref, km_ref, carry_ref):
    tm = x_ref.shape[1]

    @pl.when(pl.program_id(1) == 0)
    def _():
        carry_ref[...] = jnp.zeros_like(carry_ref)

    mod = mod_ref[0]
    h = _rmsnorm(x_ref[0], g1_ref[...]) * (1.0 + mod[1:2]) + mod[0:1]
    u = jnp.dot(h.astype(BF16), win_ref[...], preferred_element_type=F32)

    cw = CONV_WIDTH
    u_b, gated = u[:, :cw], u[:, cw:2 * cw] * u[:, 2 * cw:3 * cw]
    row = lax.broadcasted_iota(jnp.int32, (tm, 1), 0)
    tail = carry_ref[...]
    prev1 = jnp.where(row == 0, tail[7:8], pltpu.roll(gated, 1, 0))
    prev2 = jnp.where(row == 0, tail[6:7], jnp.where(row == 1, tail[7:8], pltpu.roll(gated, 2, 0)))
    taps = cw_ref[...]
    y_conv = u_b * (taps[0:1] * prev2 + taps[1:2] * prev1 + taps[2:3] * gated)
    carry_ref[...] = gated[tm - 8:]
    yc_ref[0] = _rmsnorm(y_conv, cg_ref[...]).astype(BF16)

    a0 = 3 * cw
    cos_t, sin_t = cos_ref[0], sin_ref[0]
    q_t = _headnorm_rope_t(u[:, a0:a0 + ATT_WIDTH].T, qg_ref[...], cos_t, sin_t)
    k_t = _headnorm_rope_t(u[:, a0 + ATT_WIDTH:a0 + 2 * ATT_WIDTH].T, kg_ref[...], cos_t, sin_t)
    k = k_t.T
    qt_ref[0, 0] = (q_t * (HEAD_DIM ** -0.5)).astype(BF16)
    k_ref[0, 0] = k.astype(BF16)
    km_ref[0, 0] = jnp.mean(k, axis=0, keepdims=True)
    vt_ref[0, 0] = u[:, a0 + 2 * ATT_WIDTH:].T.astype(BF16)


def _mixer_in(x, mod_l, g1, w_in, conv_w, conv_g, q_g, k_g, cos_t, sin_t):
    bsz, seq, d = x.shape
    tm = TOKEN_TILE
    nb = seq // tm
    const2 = lambda b, t: (0, 0)
    blocked = pl.BlockSpec((1, 1, ATT_WIDTH, tm), lambda b, t: (b, t, 0, 0))
    return pl.pallas_call(
        _mixer_in_kernel,
        grid=(bsz, nb),
        in_specs=[
            pl.BlockSpec((1, tm, d), lambda b, t: (b, t, 0)),
            pl.BlockSpec((1, 6, d), lambda b, t: (b, 0, 0)),
            pl.BlockSpec((1, d), const2),
            pl.BlockSpec((d, IN_WIDTH), const2),
            pl.BlockSpec((CONV_K, CONV_WIDTH), const2),
            pl.BlockSpec((1, CONV_WIDTH), const2),
            pl.BlockSpec((HEAD_DIM, tm), const2),
            pl.BlockSpec((HEAD_DIM, tm), const2),
            pl.BlockSpec((1, HALF_DIM, tm), lambda b, t: (b, 0, t)),
            pl.BlockSpec((1, HALF_DIM, tm), lambda b, t: (b, 0, t)),
        ],
        out_specs=[
            pl.BlockSpec((1, tm, CONV_WIDTH), lambda b, t: (b, t, 0)),
            blocked,
            pl.BlockSpec((1, 1, tm, ATT_WIDTH), lambda b, t: (b, t, 0, 0)),
            blocked,
            pl.BlockSpec((1, 1, 1, ATT_WIDTH), lambda b, t: (b, t, 0, 0)),
        ],
        out_shape=[
            jax.ShapeDtypeStruct((bsz, seq, CONV_WIDTH), BF16),
            jax.ShapeDtypeStruct((bsz, nb, ATT_WIDTH, tm), BF16),
            jax.ShapeDtypeStruct((bsz, nb, tm, ATT_WIDTH), BF16),
            jax.ShapeDtypeStruct((bsz, nb, ATT_WIDTH, tm), BF16),
            jax.ShapeDtypeStruct((bsz, nb, 1, ATT_WIDTH), F32),
        ],
        scratch_shapes=[pltpu.VMEM((8, CONV_WIDTH), F32)],
        compiler_params=_params(2),
        name="mixer_in",
    )(x, mod_l, g1, w_in, conv_w, conv_g, q_g, k_g, cos_t, sin_t)


def _attn_kernel(qt_ref, k_ref, vt_ref, km_ref, o_ref, bias_ref, m_ref, l_ref, acc_ref):
    own = pl.program_id(2)
    nb = k_ref.shape[1]
    tq = qt_ref.shape[-1]
    n_sel = max(1, min(MOBA_TOPK, nb - 1))

    q_t = qt_ref[0, 0]
    dead = jnp.zeros((HEAD_DIM, tq), BF16)
    q_heads = (jnp.concatenate([q_t[:HEAD_DIM], dead], axis=0),
               jnp.concatenate([dead, q_t[HEAD_DIM:]], axis=0))

    km = km_ref[0]
    km_hi = km.astype(BF16).astype(F32)
    km_lo = km - km_hi
    first = lax.broadcasted_iota(jnp.int32, (1, HEAD_PAIR), 1) < HEAD_DIM
    km_rows = jnp.concatenate([jnp.where(first, km_hi, 0.0), jnp.where(first, 0.0, km_hi),
                               jnp.where(first, km_lo, 0.0), jnp.where(first, 0.0, km_lo)], axis=0)
    gate = jnp.dot(km_rows.astype(BF16), q_t, preferred_element_type=F32)
    blk = lax.broadcasted_iota(jnp.int32, (nb, tq), 0)
    past = blk < own
    for h in range(2):
        g = jnp.where(past, gate[h * nb:(h + 1) * nb] + gate[(2 + h) * nb:(3 + h) * nb], NEG)
        rank = jnp.zeros((nb, tq), F32)
        for m in range(nb):
            gm = g[m:m + 1]
            ahead = (gm > g) | ((gm == g) & (blk > m))
            rank = rank + jnp.where(ahead, 1.0, 0.0)
        bias_ref[h] = jnp.where((rank < n_sel) & past, 0.0, NEG)

    k_own, vt_own = k_ref[0, own], vt_ref[0, own]
    causal = (lax.broadcasted_iota(jnp.int32, (tq, tq), 0) <= lax.broadcasted_iota(jnp.int32, (tq, tq), 1))
    for h in range(2):
        s = jnp.where(causal, jnp.dot(k_own, q_heads[h], preferred_element_type=F32), NEG)
        m = jnp.max(s, axis=0, keepdims=True)
        p = jnp.exp(s - m)
        m_ref[h] = m
        l_ref[h] = jnp.sum(p, axis=0, keepdims=True)
        acc_ref[h] = jnp.dot(vt_own, p.astype(BF16), preferred_element_type=F32)

    def past_block(n, carry):
        k_blk, vt_blk = k_ref[0, n], vt_ref[0, n]
        for h in range(2):
            s = jnp.dot(k_blk, q_heads[h], preferred_element_type=F32) + bias_ref[h, pl.ds(n, 1), :]
            m_old = m_ref[h]
            m_new = jnp.maximum(m_old, jnp.max(s, axis=0, keepdims=True))
            alpha = jnp.exp(m_old - m_new)
            p = jnp.exp(s - m_new)
            m_ref[h] = m_new
            l_ref[h] = alpha * l_ref[h] + jnp.sum(p, axis=0, keepdims=True)
            acc_ref[h] = alpha * acc_ref[h] + jnp.dot(vt_blk, p.astype(BF16), preferred_element_type=F32)
        return carry

    lax.fori_loop(0, own, past_block, 0)

    o0 = acc_ref[0] * (1.0 / l_ref[0])
    o1 = acc_ref[1] * (1.0 / l_ref[1])
    o_t = jnp.concatenate([o0[:HEAD_DIM], o1[HEAD_DIM:]], axis=0)
    o_ref[0] = o_t.T.astype(BF16)


def _attention(q_t, k, v_t, kmean):
    bsz, nb, _, tq = q_t.shape
    seq = nb * tq
    return pl.pallas_call(
        _attn_kernel,
        grid=(bsz, N_PAIRS, nb),
        in_specs=[
            pl.BlockSpec((1, 1, HEAD_PAIR, tq), lambda b, p, i: (b, i, p, 0)),
            pl.BlockSpec((1, nb, tq, HEAD_PAIR), lambda b, p, i: (b, 0, 0, p)),
            pl.BlockSpec((1, nb, HEAD_PAIR, tq), lambda b, p, i: (b, 0, p, 0)),
            pl.BlockSpec((1, nb, HEAD_PAIR), lambda b, p, i: (b, 0, p)),
        ],
        out_specs=pl.BlockSpec((1, tq, HEAD_PAIR), lambda b, p, i: (b, i, p)),
        out_shape=jax.ShapeDtypeStruct((bsz, seq, ATT_WIDTH), BF16),
        scratch_shapes=[
            pltpu.VMEM((2, nb, tq), F32),
            pltpu.VMEM((2, 1, tq), F32),
            pltpu.VMEM((2, 1, tq), F32),
            pltpu.VMEM((2, HEAD_PAIR, tq), F32),
        ],
        compiler_params=_params(3),
        name="moba_attention",
    )(q_t, k, v_t, kmean)


def _mixer_out_kernel(x_ref, yc_ref, ya_ref, mod_ref, ag_ref, wout_ref, g2_ref, wr_ref, br_ref, tri_ref,
                      x1_ref, hs_ref, meta_ref, cnt_ref):
    @pl.when((pl.program_id(0) == 0) & (pl.program_id(1) == 0))
    def _():
        cnt_ref[...] = jnp.zeros_like(cnt_ref)

    mod = mod_ref[0]
    ya = _rmsnorm(ya_ref[0].astype(F32), ag_ref[...]).astype(BF16)
    proj = (jnp.dot(yc_ref[0], wout_ref[:CONV_WIDTH], preferred_element_type=F32)
            + jnp.dot(ya, wout_ref[CONV_WIDTH:], preferred_element_type=F32))
    x1 = x_ref[0] + mod[2:3] * proj
    x1_ref[0] = x1
    h2 = _rmsnorm(x1, g2_ref[...]) * (1.0 + mod[4:5]) + mod[3:4]

    logits = jnp.dot(h2.astype(BF16), wr_ref[...], preferred_element_type=F32) + br_ref[...]
    lane = lax.broadcasted_iota(jnp.int32, (1, LANES), 1).astype(F32)
    is_group = lane < N_GROUPS
    gl = jnp.where(is_group, logits, LOWEST)
    g_max = jnp.max(gl, axis=-1, keepdims=True)
    g_sel = jnp.min(jnp.where(gl == g_max, lane, float(LANES)), axis=-1, keepdims=True)
    p_group = 1.0 / jnp.sum(jnp.where(is_group, jnp.exp(gl - g_max), 0.0), axis=-1, keepdims=True)

    e0 = N_GROUPS + EXPERTS_PER_GROUP * g_sel
    el = jnp.where((lane >= e0) & (lane < e0 + EXPERTS_PER_GROUP), logits, LOWEST)
    v1 = jnp.max(el, axis=-1, keepdims=True)
    i1 = jnp.min(jnp.where(el == v1, lane, float(LANES)), axis=-1, keepdims=True)
    el2 = jnp.where(lane == i1, LOWEST, el)
    v2 = jnp.max(el2, axis=-1, keepdims=True)
    i2 = jnp.min(jnp.where(el2 == v2, lane, float(LANES)), axis=-1, keepdims=True)
    ex = jnp.exp(v2 - v1)
    w_top1 = p_group / (1.0 + ex)
    w_top2 = w_top1 * ex

    a, b = i1 - e0, i2 - e0
    lo, hi = jnp.minimum(a, b), jnp.maximum(a, b)
    bucket = g_sel * PAIRS_PER_GROUP + lo * (7.0 - lo) * 0.5 + (hi - lo - 1.0)
    w_lo = jnp.where(a < b, w_top1, w_top2)
    w_hi = jnp.where(a < b, w_top2, w_top1)

    onehot = jnp.where(lane == bucket, 1.0, 0.0)
    before = jnp.dot(tri_ref[...], onehot.astype(BF16), preferred_element_type=F32)
    cnt = cnt_ref[...]
    rank = jnp.sum((before + cnt) * onehot, axis=-1, keepdims=True)
    cnt_ref[...] = cnt + jnp.sum(onehot, axis=0, keepdims=True)

    meta_ref[0] = jnp.where(lane == 0.0, bucket, jnp.where(lane == 1.0, rank, 0.0))
    hs_ref[0, :, :D_MODEL] = h2
    hs_ref[0, :, D_MODEL:] = jnp.where(lane == 0.0, w_lo, jnp.where(lane == 1.0, w_hi, 0.0))


def _mixer_out(x, yc, ya, mod_l, attn_g, w_out, g2, w_router, b_router, tri):
    bsz, seq, d = x.shape
    tm = TOKEN_TILE
    const2 = lambda b, t: (0, 0)
    tok = lambda width: pl.BlockSpec((1, tm, width), lambda b, t: (b, t, 0))
    return pl.pallas_call(
        _mixer_out_kernel,
        grid=(bsz, seq // tm),
        in_specs=[
            tok(d), tok(CONV_WIDTH), tok(ATT_WIDTH),
            pl.BlockSpec((1, 6, d), lambda b, t: (b, 0, 0)),
            pl.BlockSpec((1, ATT_WIDTH), const2),
            pl.BlockSpec((d, d), const2),
            pl.BlockSpec((1, d), const2),
            pl.BlockSpec((d, LANES), const2),
            pl.BlockSpec((1, LANES), const2),
            pl.BlockSpec((tm, tm), const2),
        ],
        out_specs=[tok(d), tok(ROW_WORDS), tok(LANES), pl.BlockSpec((1, LANES), const2)],
        out_shape=[
            jax.ShapeDtypeStruct((bsz, seq, d), F32),
            jax.ShapeDtypeStruct((bsz, seq, ROW_WORDS), F32),
            jax.ShapeDtypeStruct((bsz, seq, LANES), F32),
            jax.ShapeDtypeStruct((1, LANES), F32),
        ],
        compiler_params=_params(2),
        name="mixer_out",
    )(x, yc, ya, mod_l, attn_g, w_out, g2, w_router, b_router, tri)


def _dispatch_kernel(pos_ref, src_ref, dst_in_ref, dst_ref, sem):
    del dst_in_ref
    rows = pos_ref.shape[-1]
    base = pl.program_id(0) * rows

    def row_copy(src_row, dst_row):
        return pltpu.make_async_copy(src_ref.at[pl.ds(src_row, 1)], dst_ref.at[pl.ds(dst_row, 1)], sem)

    def issue(r, carry):
        row_copy(base + r, pos_ref[0, 0, r]).start()
        return carry

    def drain(r, carry):
        row_copy(base + r, pos_ref[0, 0, r]).wait()
        return carry

    lax.fori_loop(0, rows, issue, 0)
    lax.fori_loop(0, rows, drain, 0)


def _dispatch(pos, rows_tok, n_sorted):
    n, width = rows_tok.shape
    mt = MOVE_TILE
    return pl.pallas_call(
        _dispatch_kernel,
        grid=(n // mt,),
        in_specs=[
            pl.BlockSpec((1, 1, mt), lambda i: (i, 0, 0), memory_space=pltpu.SMEM),
            pl.BlockSpec(memory_space=pl.ANY),
            pl.BlockSpec(memory_space=pl.ANY),
        ],
        out_specs=pl.BlockSpec(memory_space=pl.ANY),
        out_shape=jax.ShapeDtypeStruct((n_sorted, width), F32),
        scratch_shapes=[pltpu.SemaphoreType.DMA(())],
        input_output_aliases={2: 0},
        compiler_params=_params(1),
        name="moe_dispatch",
    )(pos.reshape(n // mt, 1, mt), rows_tok, jnp.zeros((n_sorted, width), F32))


def _expert_kernel(lo_ref, hi_ref, live_ref, hs_ref, wgu_lo, wgu_hi, wd_lo, wd_hi, y_ref):
    del lo_ref, hi_ref
    t = pl.program_id(0)

    @pl.when(live_ref[t] != 0)
    def _():
        hs = hs_ref[...]
        h = hs[:, :D_MODEL].astype(BF16)

        def expert(wgu_ref, wd_ref, w):
            gu = jnp.dot(h, wgu_ref[0], preferred_element_type=F32)
            hid = _silu(gu[:, :D_FF_EXPERT]) * gu[:, D_FF_EXPERT:] * w
            return jnp.dot(hid.astype(BF16), wd_ref[0], preferred_element_type=F32)

        y_ref[...] = (expert(wgu_lo, wd_lo, hs[:, D_MODEL:D_MODEL + 1])
                      + expert(wgu_hi, wd_hi, hs[:, D_MODEL + 1:D_MODEL + 2]))

    @pl.when(live_ref[t] == 0)
    def _():
        y_ref[...] = jnp.zeros_like(y_ref)


def _experts(tile_lo, tile_hi, tile_live, rows_sorted, w_gu, w_down):
    n_sorted = rows_sorted.shape[0]
    te = EXPERT_TILE
    grid_spec = pltpu.PrefetchScalarGridSpec(
        num_scalar_prefetch=3,
        grid=(n_sorted // te,),
        in_specs=[
            pl.BlockSpec((te, ROW_WORDS), lambda t, lo, hi, live: (t, 0)),
            pl.BlockSpec((1, D_MODEL, 2 * D_FF_EXPERT), lambda t, lo, hi, live: (lo[t], 0, 0)),
            pl.BlockSpec((1, D_MODEL, 2 * D_FF_EXPERT), lambda t, lo, hi, live: (hi[t], 0, 0)),
            pl.BlockSpec((1, D_FF_EXPERT, D_MODEL), lambda t, lo, hi, live: (lo[t], 0, 0)),
            pl.BlockSpec((1, D_FF_EXPERT, D_MODEL), lambda t, lo, hi, live: (hi[t], 0, 0)),
        ],
        out_specs=pl.BlockSpec((te, D_MODEL), lambda t, lo, hi, live: (t, 0)),
    )
    return pl.pallas_call(
        _expert_kernel,
        grid_spec=grid_spec,
        out_shape=jax.ShapeDtypeStruct((n_sorted, D_MODEL), F32),
        compiler_params=_params(1),
        name="moe_experts",
    )(tile_lo, tile_hi, tile_live, rows_sorted, w_gu, w_gu, w_down, w_down)


def _combine_kernel(pos_ref, x1_ref, mod_ref, ys_ref, o_ref, buf, sem):
    rows = pos_ref.shape[-1]

    def row_copy(r):
        return pltpu.make_async_copy(ys_ref.at[pl.ds(pos_ref[0, 0, r], 1)], buf.at[pl.ds(r, 1)], sem)

    def issue(r, carry):
        row_copy(r).start()
        return carry

    def drain(r, carry):
        row_copy(r).wait()
        return carry

    lax.fori_loop(0, rows, issue, 0)
    lax.fori_loop(0, rows, drain, 0)
    o_ref[0] = x1_ref[0] + mod_ref[0][5:6] * buf[...]


def _combine(pos, x1, mod_l, y_sorted):
    bsz, seq, d = x1.shape
    mt = MOVE_TILE
    per_seq = seq // mt
    return pl.pallas_call(
        _combine_kernel,
        grid=(bsz, per_seq),
        in_specs=[
            pl.BlockSpec((1, 1, mt), lambda b, t: (b * per_seq + t, 0, 0), memory_space=pltpu.SMEM),
            pl.BlockSpec((1, mt, d), lambda b, t: (b, t, 0)),
            pl.BlockSpec((1, 6, d), lambda b, t: (b, 0, 0)),
            pl.BlockSpec(memory_space=pl.ANY),
        ],
        out_specs=pl.BlockSpec((1, mt, d), lambda b, t: (b, t, 0)),
        out_shape=jax.ShapeDtypeStruct((bsz, seq, d), F32),
        scratch_shapes=[pltpu.VMEM((mt, d), F32), pltpu.SemaphoreType.DMA(())],
        compiler_params=_params(2),
        name="moe_combine",
    )(pos.reshape(bsz * per_seq, 1, mt), x1, mod_l, y_sorted)


def _routing_tables(meta, counts, n_tiles):
    te = EXPERT_TILE
    bucket = meta[:, 0].astype(jnp.int32)
    rank = meta[:, 1].astype(jnp.int32)
    counts = counts[0, :N_BUCKETS].astype(jnp.int32)
    tiles_per = (counts + te - 1) // te
    tile_end = jnp.cumsum(tiles_per)
    offsets = (tile_end - tiles_per) * te
    pos = offsets[bucket] + rank
    tile_ids = jnp.arange(n_tiles, dtype=jnp.int32)
    live = tile_ids < tile_end[-1]
    last_live = jnp.maximum(tile_end[-1] - 1, 0)
    tile_bucket = jnp.searchsorted(tile_end, jnp.minimum(tile_ids, last_live), side="right").astype(jnp.int32)
    tile_bucket = jnp.minimum(tile_bucket, N_BUCKETS - 1)
    pair_lo = jnp.array([0, 0, 0, 1, 1, 2], jnp.int32)
    pair_hi = jnp.array([1, 2, 3, 2, 3, 3], jnp.int32)
    group0 = (tile_bucket // PAIRS_PER_GROUP) * EXPERTS_PER_GROUP
    pair = tile_bucket % PAIRS_PER_GROUP
    return pos, group0 + pair_lo[pair], group0 + pair_hi[pair], live.astype(jnp.int32)


def kernel(x, c, positions, norm1_g, norm2_g, w_ada, b_ada, w_in, conv_w, q_norm_g, k_norm_g, conv_out_g,
           attn_out_g, w_out, w_router_group, b_router_group, w_router_expert, b_router_expert,
           w_gate, w_up, w_down):
    bsz, seq, d = x.shape
    depth = w_in.shape[0]
    n = bsz * seq
    tm = TOKEN_TILE
    n_tiles = n // EXPERT_TILE + N_BUCKETS
    n_sorted = n_tiles * EXPERT_TILE

    mod = _modulation(c, w_ada, b_ada).reshape(depth, bsz, 6, d)
    cos_t, sin_t = _rope_tables(positions)

    w_in_b = w_in.astype(BF16)
    w_out_b = w_out.astype(BF16)
    w_gu_b = jnp.concatenate([w_gate, w_up], axis=-1).astype(BF16)
    w_down_b = w_down.astype(BF16)
    pad = LANES - N_GROUPS - N_EXPERTS
    w_router = jnp.concatenate(
        [w_router_group, w_router_expert, jnp.zeros((depth, d, pad), F32)], axis=-1).astype(BF16)
    b_router = jnp.concatenate(
        [b_router_group, b_router_expert, jnp.zeros((depth, pad), F32)], axis=-1).reshape(depth, 1, LANES)
    q_gain = jnp.broadcast_to(q_norm_g[:, :, None], (depth, HEAD_DIM, tm))
    k_gain = jnp.broadcast_to(k_norm_g[:, :, None], (depth, HEAD_DIM, tm))
    idx = jnp.arange(tm)
    tri = (idx[None, :] < idx[:, None]).astype(BF16)

    for l in range(depth):
        yc, q_t, k, v_t, kmean = _mixer_in(
            x, mod[l], norm1_g[l][None], w_in_b[l], conv_w[l], conv_out_g[l][None],
            q_gain[l], k_gain[l], cos_t, sin_t)
        ya = _attention(q_t, k, v_t, kmean.reshape(bsz, seq // tm, ATT_WIDTH))
        x1, rows_tok, meta, counts = _mixer_out(
            x, yc, ya, mod[l], attn_out_g[l][None], w_out_b[l], norm2_g[l][None],
            w_router[l], b_router[l], tri)
        pos, tile_lo, tile_hi, tile_live = _routing_tables(meta.reshape(n, LANES), counts, n_tiles)
        rows_sorted = _dispatch(pos, rows_tok.reshape(n, ROW_WORDS), n_sorted)
        y_sorted = _experts(tile_lo, tile_hi, tile_live, rows_sorted, w_gu_b[l], w_down_b[l])
        x = _combine(pos, x1, mod[l], y_sorted)
    return x
```

```python
import functools

import jax
import jax.numpy as jnp
from jax import lax
from jax.experimental import pallas as pl
from jax.experimental.pallas import tpu as pltpu

F32 = jnp.float32
BF16 = jnp.bfloat16

D_MODEL = 1024
CONV_WIDTH = 512
CONV_K = 3
HEAD_DIM = 64
HALF_DIM = HEAD_DIM // 2
ATT_WIDTH = 512
ATT_HEADS = ATT_WIDTH // HEAD_DIM
IN_WIDTH = 3 * CONV_WIDTH + 3 * ATT_WIDTH
MOBA_BLOCK = 256
MOBA_TOPK = 3
ROPE_THETA = 10000.0
N_GROUPS = 4
EXPERTS_PER_GROUP = 4
N_EXPERTS = N_GROUPS * EXPERTS_PER_GROUP
D_FF_EXPERT = D_MODEL // 4
EPS = 1e-6
NEG = -1e30
LOWEST = -3e38

LANES = 128
HEAD_PAIR = 2 * HEAD_DIM
N_PAIRS = ATT_WIDTH // HEAD_PAIR
TOKEN_TILE = MOBA_BLOCK
PAIRS_PER_GROUP = 6
N_BUCKETS = N_GROUPS * PAIRS_PER_GROUP
EXPERT_TILE = 256
ROW_WORDS = D_MODEL + LANES
MOVE_TILE = 512
VMEM_LIMIT = 48 * 1024 * 1024


def _params(n_axes, vmem=VMEM_LIMIT):
    return pltpu.CompilerParams(dimension_semantics=("arbitrary",) * n_axes, vmem_limit_bytes=vmem)


def _rmsnorm(x, g):
    return x * lax.rsqrt(jnp.mean(x * x, axis=-1, keepdims=True) + EPS) * g


def _silu(x):
    return x * (1.0 / (1.0 + jnp.exp(-x)))


def _mod_kernel(c_ref, w_ref, b_ref, o_ref):
    ca = _silu(c_ref[...]).astype(BF16)
    o_ref[0] = jnp.dot(ca, w_ref[0].astype(BF16), preferred_element_type=F32) + b_ref[0]


def _modulation(c, w_ada, b_ada):
    depth, d, width = w_ada.shape
    bsz = c.shape[0]
    tn = 1536
    return pl.pallas_call(
        _mod_kernel,
        grid=(depth, width // tn),
        in_specs=[
            pl.BlockSpec((bsz, d), lambda l, j: (0, 0)),
            pl.BlockSpec((1, d, tn), lambda l, j: (l, 0, j)),
            pl.BlockSpec((1, 1, tn), lambda l, j: (l, 0, j)),
        ],
        out_specs=pl.BlockSpec((1, bsz, tn), lambda l, j: (l, 0, j)),
        out_shape=jax.ShapeDtypeStruct((depth, bsz, width), F32),
        compiler_params=_params(2),
        name="modulation",
    )(c, w_ada, b_ada.reshape(depth, 1, width))


def _rope_kernel(pos_ref, freq_ref, cos_ref, sin_ref):
    ts = pos_ref.shape[-1]
    freq = jnp.concatenate([freq_ref[...]] * (ts // LANES), axis=1)
    ang = pos_ref[0] * freq
    cos_ref[0] = jnp.cos(ang)
    sin_ref[0] = jnp.sin(ang)


def _rope_tables(positions):
    bsz, seq = positions.shape
    ts = 512
    inv_freq = ROPE_THETA ** (-jnp.arange(0, HEAD_DIM, 2, dtype=F32) / HEAD_DIM)
    freq = jnp.broadcast_to(inv_freq[:, None], (HALF_DIM, LANES))
    pos = positions.astype(F32).reshape(bsz, 1, seq)
    spec = pl.BlockSpec((1, HALF_DIM, ts), lambda b, t: (b, 0, t))
    return pl.pallas_call(
        _rope_kernel,
        grid=(bsz, seq // ts),
        in_specs=[pl.BlockSpec((1, 1, ts), lambda b, t: (b, 0, t)),
                  pl.BlockSpec((HALF_DIM, LANES), lambda b, t: (0, 0))],
        out_specs=[spec, spec],
        out_shape=[jax.ShapeDtypeStruct((bsz, HALF_DIM, seq), F32)] * 2,
        compiler_params=_params(2),
        name="rope_tables",
    )(pos, freq)


def _headnorm_rope_t(z_t, gain, cos_t, sin_t):
    outs = []
    for hd in range(ATT_HEADS):
        z = z_t[hd * HEAD_DIM:(hd + 1) * HEAD_DIM]
        zn = z * lax.rsqrt(jnp.mean(z * z, axis=0, keepdims=True) + EPS) * gain
        x1, x2 = zn[:HALF_DIM], zn[HALF_DIM:]
        outs.append(x1 * cos_t - x2 * sin_t)
        outs.append(x2 * cos_t + x1 * sin_t)
    return jnp.concatenate(outs, axis=0)


def _mixer_in_kernel(x_ref, mod_ref, g1_ref, win_ref, cw_ref, cg_ref, qg_ref, kg_ref, cos_ref, sin_ref,
                     yc_ref, qt_ref, k_ref, vt_ref, km_ref, carry_ref):
    tm = x_ref.shape[1]

    @pl.when(pl.program_id(1) == 0)
    def _():
        carry_ref[...] = jnp.zeros_like(carry_ref)

    mod = mod_ref[0]
    h = _rmsnorm(x_ref[0], g1_ref[...]) * (1.0 + mod[1:2]) + mod[0:1]
    u = jnp.dot(h.astype(BF16), win_ref[...], preferred_element_type=F32)

    cw = CONV_WIDTH
    u_b, gated = u[:, :cw], u[:, cw:2 * cw] * u[:, 2 * cw:3 * cw]
    row = lax.broadcasted_iota(jnp.int32, (tm, 1), 0)
    tail = carry_ref[...]
    prev1 = jnp.where(row == 0, tail[7:8], pltpu.roll(gated, 1, 0))
    prev2 = jnp.where(row == 0, tail[6:7], jnp.where(row == 1, tail[7:8], pltpu.roll(gated, 2, 0)))
    taps = cw_ref[...]
    y_conv = u_b * (taps[0:1] * prev2 + taps[1:2] * prev1 + taps[2:3] * gated)
    carry_ref[...] = gated[tm - 8:]
    yc_ref[0] = _rmsnorm(y_conv, cg_ref[...]).astype(BF16)

    a0 = 3 * cw
    cos_t, sin_t = cos_ref[0], sin_ref[0]
    q_t = _headnorm_rope_t(u[:, a0:a0 + ATT_WIDTH].T, qg_ref[...], cos_t, sin_t)
    k_t = _headnorm_rope_t(u[:, a0 + ATT_WIDTH:a0 + 2 * ATT_WIDTH].T, kg_ref[...], cos_t, sin_t)
    k = k_t.T
    qt_ref[0, 0] = (q_t * (HEAD_DIM ** -0.5)).astype(BF16)
    k_ref[0, 0] = k.astype(BF16)
    km_ref[0, 0] = jnp.mean(k, axis=0, keepdims=True)
    vt_ref[0, 0] = u[:, a0 + 2 * ATT_WIDTH:].T.astype(BF16)


def _mixer_in(x, mod_l, g1, w_in, conv_w, conv_g, q_g, k_g, cos_t, sin_t):
    bsz, seq, d = x.shape
    tm = TOKEN_TILE
    nb = seq // tm
    const2 = lambda b, t: (0, 0)
    blocked = pl.BlockSpec((1, 1, ATT_WIDTH, tm), lambda b, t: (b, t, 0, 0))
    return pl.pallas_call(
        _mixer_in_kernel,
        grid=(bsz, nb),
        in_specs=[
            pl.BlockSpec((1, tm, d), lambda b, t: (b, t, 0)),
            pl.BlockSpec((1, 6, d), lambda b, t: (b, 0, 0)),
            pl.BlockSpec((1, d), const2),
            pl.BlockSpec((d, IN_WIDTH), const2),
            pl.BlockSpec((CONV_K, CONV_WIDTH), const2),
            pl.BlockSpec((1, CONV_WIDTH), const2),
            pl.BlockSpec((HEAD_DIM, tm), const2),
            pl.BlockSpec((HEAD_DIM, tm), const2),
            pl.BlockSpec((1, HALF_DIM, tm), lambda b, t: (b, 0, t)),
            pl.BlockSpec((1, HALF_DIM, tm), lambda b, t: (b, 0, t)),
        ],
        out_specs=[
            pl.BlockSpec((1, tm, CONV_WIDTH), lambda b, t: (b, t, 0)),
            blocked,
            pl.BlockSpec((1, 1, tm, ATT_WIDTH), lambda b, t: (b, t, 0, 0)),
            blocked,
            pl.BlockSpec((1, 1, 1, ATT_WIDTH), lambda b, t: (b, t, 0, 0)),
        ],
        out_shape=[
            jax.ShapeDtypeStruct((bsz, seq, CONV_WIDTH), BF16),
            jax.ShapeDtypeStruct((bsz, nb, ATT_WIDTH, tm), BF16),
            jax.ShapeDtypeStruct((bsz, nb, tm, ATT_WIDTH), BF16),
            jax.ShapeDtypeStruct((bsz, nb, ATT_WIDTH, tm), BF16),
            jax.ShapeDtypeStruct((bsz, nb, 1, ATT_WIDTH), F32),
        ],
        scratch_shapes=[pltpu.VMEM((8, CONV_WIDTH), F32)],
        compiler_params=_params(2),
        name="mixer_in",
    )(x, mod_l, g1, w_in, conv_w, conv_g, q_g, k_g, cos_t, sin_t)


def _attn_kernel(qt_ref, k_ref, vt_ref, km_ref, o_ref, bias_ref, m_ref, l_ref, acc_ref):
    own = pl.program_id(2)
    nb = k_ref.shape[1]
    tq = qt_ref.shape[-1]
    n_sel = max(1, min(MOBA_TOPK, nb - 1))

    q_t = qt_ref[0, 0]
    dead = jnp.zeros((HEAD_DIM, tq), BF16)
    q_heads = (jnp.concatenate([q_t[:HEAD_DIM], dead], axis=0),
               jnp.concatenate([dead, q_t[HEAD_DIM:]], axis=0))

    km = km_ref[0]
    km_hi = km.astype(BF16).astype(F32)
    km_lo = km - km_hi
    first = lax.broadcasted_iota(jnp.int32, (1, HEAD_PAIR), 1) < HEAD_DIM
    km_rows = jnp.concatenate([jnp.where(first, km_hi, 0.0), jnp.where(first, 0.0, km_hi),
                               jnp.where(first, km_lo, 0.0), jnp.where(first, 0.0, km_lo)], axis=0)
    gate = jnp.dot(km_rows.astype(BF16), q_t, preferred_element_type=F32)
    blk = lax.broadcasted_iota(jnp.int32, (nb, tq), 0)
    past = blk < own
    for h in range(2):
        g = jnp.where(past, gate[h * nb:(h + 1) * nb] + gate[(2 + h) * nb:(3 + h) * nb], NEG)
        rank = jnp.zeros((nb, tq), F32)
        for m in range(nb):
            gm = g[m:m + 1]
            ahead = (gm > g) | ((gm == g) & (blk > m))
            rank = rank + jnp.where(ahead, 1.0, 0.0)
        bias_ref[h] = jnp.where((rank < n_sel) & past, 0.0, NEG)

    k_own, vt_own = k_ref[0, own], vt_ref[0, own]
    causal = (lax.broadcasted_iota(jnp.int32, (tq, tq), 0) <= lax.broadcasted_iota(jnp.int32, (tq, tq), 1))
    for h in range(2):
        s = jnp.where(causal, jnp.dot(k_own, q_heads[h], preferred_element_type=F32), NEG)
        m = jnp.max(s, axis=0, keepdims=True)
        p = jnp.exp(s - m)
        m_ref[h] = m
        l_ref[h] = jnp.sum(p, axis=0, keepdims=True)
        acc_ref[h] = jnp.dot(vt_own, p.astype(BF16), preferred_element_type=F32)

    def past_block(n, carry):
        k_blk, vt_blk = k_ref[0, n], vt_ref[0, n]
        for h in range(2):
            s = jnp.dot(k_blk, q_heads[h], preferred_element_type=F32) + bias_ref[h, pl.ds(n, 1), :]
            m_old = m_ref[h]
            m_new = jnp.maximum(m_old, jnp.max(s, axis=0, keepdims=True))
            alpha = jnp.exp(m_old - m_new)
            p = jnp.exp(s - m_new)
            m_ref[h] = m_new
            l_ref[h] = alpha * l_ref[h] + jnp.sum(p, axis=0, keepdims=True)
            acc_ref[h] = alpha * acc_ref[h] + jnp.dot(vt_blk, p.astype(BF16), preferred_element_type=F32)
        return carry

    lax.fori_loop(0, own, past_block, 0)

    o0 = acc_ref[0] * (1.0 / l_ref[0])
    o1 = acc_ref[1] * (1.0 / l_ref[1])
    o_t = jnp.concatenate([o0[:HEAD_DIM], o1[HEAD_DIM:]], axis=0)
    o_ref[0] = o_t.T.astype(BF16)


def _attention(q_t, k, v_t, kmean):
    bsz, nb, _, tq = q_t.shape
    seq = nb * tq
    return pl.pallas_call(
        _attn_kernel,
        grid=(bsz, N_PAIRS, nb),
        in_specs=[
            pl.BlockSpec((1, 1, HEAD_PAIR, tq), lambda b, p, i: (b, i, p, 0)),
            pl.BlockSpec((1, nb, tq, HEAD_PAIR), lambda b, p, i: (b, 0, 0, p)),
            pl.BlockSpec((1, nb, HEAD_PAIR, tq), lambda b, p, i: (b, 0, p, 0)),
            pl.BlockSpec((1, nb, HEAD_PAIR), lambda b, p, i: (b, 0, p)),
        ],
        out_specs=pl.BlockSpec((1, tq, HEAD_PAIR), lambda b, p, i: (b, i, p)),
        out_shape=jax.ShapeDtypeStruct((bsz, seq, ATT_WIDTH), BF16),
        scratch_shapes=[
            pltpu.VMEM((2, nb, tq), F32),
            pltpu.VMEM((2, 1, tq), F32),
            pltpu.VMEM((2, 1, tq), F32),
            pltpu.VMEM((2, HEAD_PAIR, tq), F32),
        ],
        compiler_params=_params(3),
        name="moba_attention",
    )(q_t, k, v_t, kmean)


def _mixer_out_kernel(x_ref, yc_ref, ya_ref, mod_ref, ag_ref, wout_ref, g2_ref, wr_ref, br_ref, tri_ref,
                      x1_ref, hs_ref, meta_ref, cnt_ref):
    @pl.when((pl.program_id(0) == 0) & (pl.program_id(1) == 0))
    def _():
        cnt_ref[...] = jnp.zeros_like(cnt_ref)

    mod = mod_ref[0]
    ya = _rmsnorm(ya_ref[0].astype(F32), ag_ref[...]).astype(BF16)
    proj = (jnp.dot(yc_ref[0], wout_ref[:CONV_WIDTH], preferred_element_type=F32)
            + jnp.dot(ya, wout_ref[CONV_WIDTH:], preferred_element_type=F32))
    x1 = x_ref[0] + mod[2:3] * proj
    x1_ref[0] = x1
    h2 = _rmsnorm(x1, g2_ref[...]) * (1.0 + mod[4:5]) + mod[3:4]

    logits = jnp.dot(h2.astype(BF16), wr_ref[...], preferred_element_type=F32) + br_ref[...]
    lane = lax.broadcasted_iota(jnp.int32, (1, LANES), 1).astype(F32)
    is_group = lane < N_GROUPS
    gl = jnp.where(is_group, logits, LOWEST)
    g_max = jnp.max(gl, axis=-1, keepdims=True)
    g_sel = jnp.min(jnp.where(gl == g_max, lane, float(LANES)), axis=-1, keepdims=True)
    p_group = 1.0 / jnp.sum(jnp.where(is_group, jnp.exp(gl - g_max), 0.0), axis=-1, keepdims=True)

    e0 = N_GROUPS + EXPERTS_PER_GROUP * g_sel
    el = jnp.where((lane >= e0) & (lane < e0 + EXPERTS_PER_GROUP), logits, LOWEST)
    v1 = jnp.max(el, axis=-1, keepdims=True)
    i1 = jnp.min(jnp.where(el == v1, lane, float(LANES)), axis=-1, keepdims=True)
    el2 = jnp.where(lane == i1, LOWEST, el)
    v2 = jnp.max(el2, axis=-1, keepdims=True)
    i2 = jnp.min(jnp.where(el2 == v2, lane, float(LANES)), axis=-1, keepdims=True)
    ex = jnp.exp(v2 - v1)
    w_top1 = p_group / (1.0 + ex)
    w_top2 = w_top1 * ex

    a, b = i1 - e0, i2 - e0
    lo, hi = jnp.minimum(a, b), jnp.maximum(a, b)
    bucket = g_sel * PAIRS_PER_GROUP + lo * (7.0 - lo) * 0.5 + (hi - lo - 1.0)
    w_lo = jnp.where(a < b, w_top1, w_top2)
    w_hi = jnp.where(a < b, w_top2, w_top1)

    onehot = jnp.where(lane == bucket, 1.0, 0.0)
    before = jnp.dot(tri_ref[...], onehot.astype(BF16), preferred_element_type=F32)
    cnt = cnt_ref[...]
    rank = jnp.sum((before + cnt) * onehot, axis=-1, keepdims=True)
    cnt_ref[...] = cnt + jnp.sum(onehot, axis=0, keepdims=True)

    meta_ref[0] = jnp.where(lane == 0.0, bucket, jnp.where(lane == 1.0, rank, 0.0))
    hs_ref[0, :, :D_MODEL] = h2
    hs_ref[0, :, D_MODEL:] = jnp.where(lane == 0.0, w_lo, jnp.where(lane == 1.0, w_hi, 0.0))


def _mixer_out(x, yc, ya, mod_l, attn_g, w_out, g2, w_router, b_router, tri):
    bsz, seq, d = x.shape
    tm = TOKEN_TILE
    const2 = lambda b, t: (0, 0)
    tok = lambda width: pl.BlockSpec((1, tm, width), lambda b, t: (b, t, 0))
    return pl.pallas_call(
        _mixer_out_kernel,
        grid=(bsz, seq // tm),
        in_specs=[
            tok(d), tok(CONV_WIDTH), tok(ATT_WIDTH),
            pl.BlockSpec((1, 6, d), lambda b, t: (b, 0, 0)),
            pl.BlockSpec((1, ATT_WIDTH), const2),
            pl.BlockSpec((d, d), const2),
            pl.BlockSpec((1, d), const2),
            pl.BlockSpec((d, LANES), const2),
            pl.BlockSpec((1, LANES), const2),
            pl.BlockSpec((tm, tm), const2),
        ],
        out_specs=[tok(d), tok(ROW_WORDS), tok(LANES), pl.BlockSpec((1, LANES), const2)],
        out_shape=[
            jax.ShapeDtypeStruct((bsz, seq, d), F32),
            jax.ShapeDtypeStruct((bsz, seq, ROW_WORDS), F32),
            jax.ShapeDtypeStruct((bsz, seq, LANES), F32),
            jax.ShapeDtypeStruct((1, LANES), F32),
        ],
        compiler_params=_params(2),
        name="mixer_out",
    )(x, yc, ya, mod_l, attn_g, w_out, g2, w_router, b_router, tri)


def _dispatch_kernel(pos_ref, src_ref, dst_in_ref, dst_ref, sem):
    del dst_in_ref
    rows = pos_ref.shape[-1]

    def row_copy(r):
        return pltpu.make_async_copy(src_ref.at[pl.ds(r, 1)], dst_ref.at[pl.ds(pos_ref[0, 0, r], 1)], sem)

    def issue(r, carry):
        row_copy(r).start()
        return carry

    def drain(r, carry):
        row_copy(r).wait()
        return carry

    lax.fori_loop(0, rows, issue, 0)
    lax.fori_loop(0, rows, drain, 0)


def _dispatch(pos, rows_tok, n_sorted):
    n, width = rows_tok.shape
    mt = MOVE_TILE
    return pl.pallas_call(
        _dispatch_kernel,
        grid=(n // mt,),
        in_specs=[
            pl.BlockSpec((1, 1, mt), lambda i: (i, 0, 0), memory_space=pltpu.SMEM),
            pl.BlockSpec((mt, width), lambda i: (i, 0)),
            pl.BlockSpec(memory_space=pl.ANY),
        ],
        out_specs=pl.BlockSpec(memory_space=pl.ANY),
        out_shape=jax.ShapeDtypeStruct((n_sorted, width), F32),
        scratch_shapes=[pltpu.SemaphoreType.DMA(())],
        input_output_aliases={2: 0},
        compiler_params=_params(1),
        name="moe_dispatch",
    )(pos.reshape(n // mt, 1, mt), rows_tok, jnp.zeros((n_sorted, width), F32))


def _expert_kernel(lo_ref, hi_ref, live_ref, hs_ref, wgu_lo, wgu_hi, wd_lo, wd_hi, y_ref):
    del lo_ref, hi_ref
    t = pl.program_id(0)

    @pl.when(live_ref[t] != 0)
    def _():
        hs = hs_ref[...]
        h = hs[:, :D_MODEL].astype(BF16)

        def expert(wgu_ref, wd_ref, w):
            gu = jnp.dot(h, wgu_ref[0], preferred_element_type=F32)
            hid = _silu(gu[:, :D_FF_EXPERT]) * gu[:, D_FF_EXPERT:] * w
            return jnp.dot(hid.astype(BF16), wd_ref[0], preferred_element_type=F32)

        y_ref[...] = (expert(wgu_lo, wd_lo, hs[:, D_MODEL:D_MODEL + 1])
                      + expert(wgu_hi, wd_hi, hs[:, D_MODEL + 1:D_MODEL + 2]))

    @pl.when(live_ref[t] == 0)
    def _():
        y_ref[...] = jnp.zeros_like(y_ref)


def _experts(tile_lo, tile_hi, tile_live, rows_sorted, w_gu, w_down):
    n_sorted = rows_sorted.shape[0]
    te = EXPERT_TILE
    grid_spec = pltpu.PrefetchScalarGridSpec(
        num_scalar_prefetch=3,
        grid=(n_sorted // te,),
        in_specs=[
            pl.BlockSpec((te, ROW_WORDS), lambda t, lo, hi, live: (t, 0)),
            pl.BlockSpec((1, D_MODEL, 2 * D_FF_EXPERT), lambda t, lo, hi, live: (lo[t], 0, 0)),
            pl.BlockSpec((1, D_MODEL, 2 * D_FF_EXPERT), lambda t, lo, hi, live: (hi[t], 0, 0)),
            pl.BlockSpec((1, D_FF_EXPERT, D_MODEL), lambda t, lo, hi, live: (lo[t], 0, 0)),
            pl.BlockSpec((1, D_FF_EXPERT, D_MODEL), lambda t, lo, hi, live: (hi[t], 0, 0)),
        ],
        out_specs=pl.BlockSpec((te, D_MODEL), lambda t, lo, hi, live: (t, 0)),
    )
    return pl.pallas_call(
        _expert_kernel,
        grid_spec=grid_spec,
        out_shape=jax.ShapeDtypeStruct((n_sorted, D_MODEL), F32),
        compiler_params=_params(1),
        name="moe_experts",
    )(tile_lo, tile_hi, tile_live, rows_sorted, w_gu, w_gu, w_down, w_down)


def _combine_kernel(pos_ref, x1_ref, mod_ref, ys_ref, o_ref, buf, sem):
    rows = pos_ref.shape[-1]

    def row_copy(r):
        return pltpu.make_async_copy(ys_ref.at[pl.ds(pos_ref[0, 0, r], 1)], buf.at[pl.ds(r, 1)], sem)

    def issue(r, carry):
        row_copy(r).start()
        return carry

    def drain(r, carry):
        row_copy(r).wait()
        return carry

    lax.fori_loop(0, rows, issue, 0)
    lax.fori_loop(0, rows, drain, 0)
    o_ref[0] = x1_ref[0] + mod_ref[0][5:6] * buf[...]


def _combine(pos, x1, mod_l, y_sorted):
    bsz, seq, d = x1.shape
    mt = MOVE_TILE
    per_seq = seq // mt
    return pl.pallas_call(
        _combine_kernel,
        grid=(bsz, per_seq),
        in_specs=[
            pl.BlockSpec((1, 1, mt), lambda b, t: (b * per_seq + t, 0, 0), memory_space=pltpu.SMEM),
            pl.BlockSpec((1, mt, d), lambda b, t: (b, t, 0)),
            pl.BlockSpec((1, 6, d), lambda b, t: (b, 0, 0)),
            pl.BlockSpec(memory_space=pl.ANY),
        ],
        out_specs=pl.BlockSpec((1, mt, d), lambda b, t: (b, t, 0)),
        out_shape=jax.ShapeDtypeStruct((bsz, seq, d), F32),
        scratch_shapes=[pltpu.VMEM((mt, d), F32), pltpu.SemaphoreType.DMA(())],
        compiler_params=_params(2),
        name="moe_combine",
    )(pos.reshape(bsz * per_seq, 1, mt), x1, mod_l, y_sorted)


def _routing_tables(meta, counts, n_tiles):
    te = EXPERT_TILE
    bucket = meta[:, 0].astype(jnp.int32)
    rank = meta[:, 1].astype(jnp.int32)
    counts = counts[0, :N_BUCKETS].astype(jnp.int32)
    tiles_per = (counts + te - 1) // te
    tile_end = jnp.cumsum(tiles_per)
    offsets = (tile_end - tiles_per) * te
    pos = offsets[bucket] + rank
    tile_ids = jnp.arange(n_tiles, dtype=jnp.int32)
    live = tile_ids < tile_end[-1]
    last_live = jnp.maximum(tile_end[-1] - 1, 0)
    tile_bucket = jnp.sum(jnp.minimum(tile_ids, last_live)[:, None] >= tile_end[None, :], axis=1, dtype=jnp.int32)
    tile_bucket = jnp.minimum(tile_bucket, N_BUCKETS - 1)
    pair_lo = jnp.array([0, 0, 0, 1, 1, 2], jnp.int32)
    pair_hi = jnp.array([1, 2, 3, 2, 3, 3], jnp.int32)
    group0 = (tile_bucket // PAIRS_PER_GROUP) * EXPERTS_PER_GROUP
    pair = tile_bucket % PAIRS_PER_GROUP
    return pos, group0 + pair_lo[pair], group0 + pair_hi[pair], live.astype(jnp.int32)


def kernel(x, c, positions, norm1_g, norm2_g, w_ada, b_ada, w_in, conv_w, q_norm_g, k_norm_g, conv_out_g,
           attn_out_g, w_out, w_router_group, b_router_group, w_router_expert, b_router_expert,
           w_gate, w_up, w_down):
    bsz, seq, d = x.shape
    depth = w_in.shape[0]
    n = bsz * seq
    tm = TOKEN_TILE
    n_tiles = n // EXPERT_TILE + N_BUCKETS
    n_sorted = n_tiles * EXPERT_TILE

    mod = _modulation(c, w_ada, b_ada).reshape(depth, bsz, 6, d)
    cos_t, sin_t = _rope_tables(positions)

    w_in_b = w_in.astype(BF16)
    w_out_b = w_out.astype(BF16)
    w_gu_b = jnp.concatenate([w_gate, w_up], axis=-1).astype(BF16)
    w_down_b = w_down.astype(BF16)
    pad = LANES - N_GROUPS - N_EXPERTS
    w_router = jnp.concatenate(
        [w_router_group, w_router_expert, jnp.zeros((depth, d, pad), F32)], axis=-1).astype(BF16)
    b_router = jnp.concatenate(
        [b_router_group, b_router_expert, jnp.zeros((depth, pad), F32)], axis=-1).reshape(depth, 1, LANES)
    q_gain = jnp.broadcast_to(q_norm_g[:, :, None], (depth, HEAD_DIM, tm))
    k_gain = jnp.broadcast_to(k_norm_g[:, :, None], (depth, HEAD_DIM, tm))
    idx = jnp.arange(tm)
    tri = (idx[None, :] < idx[:, None]).astype(BF16)

    for l in range(depth):
        yc, q_t, k, v_t, kmean = _mixer_in(
            x, mod[l], norm1_g[l][None], w_in_b[l], conv_w[l], conv_out_g[l][None],
            q_gain[l], k_gain[l], cos_t, sin_t)
        ya = _attention(q_t, k, v_t, kmean.reshape(bsz, seq // tm, ATT_WIDTH))
        x1, rows_tok, meta, counts = _mixer_out(
            x, yc, ya, mod[l], attn_out_g[l][None], w_out_b[l], norm2_g[l][None],
            w_router[l], b_router[l], tri)
        pos, tile_lo, tile_hi, tile_live = _routing_tables(meta.reshape(n, LANES), counts, n_tiles)
        rows_sorted = _dispatch(pos, rows_tok.reshape(n, ROW_WORDS), n_sorted)
        y_sorted = _experts(tile_lo, tile_hi, tile_live, rows_sorted, w_gu_b[l], w_down_b[l])
        x = _combine(pos, x1, mod[l], y_sorted)
    return x
```

```python
import functools

import jax
import jax.numpy as jnp
from jax import lax
from jax.experimental import pallas as pl
from jax.experimental.pallas import tpu as pltpu

F32 = jnp.float32
BF16 = jnp.bfloat16

D_MODEL = 1024
CONV_WIDTH = 512
CONV_K = 3
HEAD_DIM = 64
HALF_DIM = HEAD_DIM // 2
ATT_WIDTH = 512
ATT_HEADS = ATT_WIDTH // HEAD_DIM
IN_WIDTH = 3 * CONV_WIDTH + 3 * ATT_WIDTH
MOBA_BLOCK = 256
MOBA_TOPK = 3
ROPE_THETA = 10000.0
N_GROUPS = 4
EXPERTS_PER_GROUP = 4
N_EXPERTS = N_GROUPS * EXPERTS_PER_GROUP
D_FF_EXPERT = D_MODEL // 4
EPS = 1e-6
NEG = -1e30
LOWEST = -3e38
LOG2_E = 1.4426950408889634
Q_SCALE = HEAD_DIM ** -0.5 * LOG2_E

LANES = 128
HEAD_PAIR = 2 * HEAD_DIM
N_PAIRS = ATT_WIDTH // HEAD_PAIR
TOKEN_TILE = MOBA_BLOCK
PAIRS_PER_GROUP = 6
N_BUCKETS = N_GROUPS * PAIRS_PER_GROUP
EXPERT_TILE = 256
ROW_WORDS = D_MODEL + LANES
MOVE_TILE = 512
ROW_UNROLL = 8
VMEM_LIMIT = 48 * 1024 * 1024


def _params(n_axes, vmem=VMEM_LIMIT):
    return pltpu.CompilerParams(dimension_semantics=("arbitrary",) * n_axes, vmem_limit_bytes=vmem)


def _rmsnorm(x, g):
    return x * lax.rsqrt(jnp.mean(x * x, axis=-1, keepdims=True) + EPS) * g


def _silu(x):
    return x * (1.0 / (1.0 + jnp.exp(-x)))


def _mod_kernel(c_ref, w_ref, b_ref, o_ref):
    ca = _silu(c_ref[...]).astype(BF16)
    o_ref[0] = jnp.dot(ca, w_ref[0].astype(BF16), preferred_element_type=F32) + b_ref[0]


def _modulation(c, w_ada, b_ada):
    depth, d, width = w_ada.shape
    bsz = c.shape[0]
    tn = 1536
    return pl.pallas_call(
        _mod_kernel,
        grid=(depth, width // tn),
        in_specs=[
            pl.BlockSpec((bsz, d), lambda l, j: (0, 0)),
            pl.BlockSpec((1, d, tn), lambda l, j: (l, 0, j)),
            pl.BlockSpec((1, 1, tn), lambda l, j: (l, 0, j)),
        ],
        out_specs=pl.BlockSpec((1, bsz, tn), lambda l, j: (l, 0, j)),
        out_shape=jax.ShapeDtypeStruct((depth, bsz, width), F32),
        compiler_params=_params(2),
        name="modulation",
    )(c, w_ada, b_ada.reshape(depth, 1, width))


def _rope_kernel(pos_ref, freq_ref, cos_ref, sin_ref):
    ts = pos_ref.shape[-1]
    freq = jnp.concatenate([freq_ref[...]] * (ts // LANES), axis=1)
    ang = pos_ref[0] * freq
    cos_ref[0] = jnp.cos(ang)
    sin_ref[0] = jnp.sin(ang)


def _rope_tables(positions):
    bsz, seq = positions.shape
    ts = 512
    inv_freq = ROPE_THETA ** (-jnp.arange(0, HEAD_DIM, 2, dtype=F32) / HEAD_DIM)
    freq = jnp.broadcast_to(inv_freq[:, None], (HALF_DIM, LANES))
    pos = positions.astype(F32).reshape(bsz, 1, seq)
    spec = pl.BlockSpec((1, HALF_DIM, ts), lambda b, t: (b, 0, t))
    return pl.pallas_call(
        _rope_kernel,
        grid=(bsz, seq // ts),
        in_specs=[pl.BlockSpec((1, 1, ts), lambda b, t: (b, 0, t)),
                  pl.BlockSpec((HALF_DIM, LANES), lambda b, t: (0, 0))],
        out_specs=[spec, spec],
        out_shape=[jax.ShapeDtypeStruct((bsz, HALF_DIM, seq), F32)] * 2,
        compiler_params=_params(2),
        name="rope_tables",
    )(pos, freq)


def _headnorm_rope_t(z_t, gain, cos_t, sin_t):
    outs = []
    for hd in range(ATT_HEADS):
        z = z_t[hd * HEAD_DIM:(hd + 1) * HEAD_DIM]
        zn = z * lax.rsqrt(jnp.mean(z * z, axis=0, keepdims=True) + EPS) * gain
        x1, x2 = zn[:HALF_DIM], zn[HALF_DIM:]
        outs.append(x1 * cos_t - x2 * sin_t)
        outs.append(x2 * cos_t + x1 * sin_t)
    return jnp.concatenate(outs, axis=0)


def _mixer_in_kernel(x_ref, mod_ref, g1_ref, win_ref, cw_ref, cg_ref, qg_ref, kg_ref, cos_ref, sin_ref,
                     yc_ref, qt_ref, k_ref, vt_ref, km_ref, carry_ref):
    tm = x_ref.shape[1]

    @pl.when(pl.program_id(1) == 0)
    def _():
        carry_ref[...] = jnp.zeros_like(carry_ref)

    mod = mod_ref[0]
    h = _rmsnorm(x_ref[0], g1_ref[...]) * (1.0 + mod[1:2]) + mod[0:1]
    u = jnp.dot(h.astype(BF16), win_ref[...], preferred_element_type=F32)

    cw = CONV_WIDTH
    u_b, gated = u[:, :cw], u[:, cw:2 * cw] * u[:, 2 * cw:3 * cw]
    row = lax.broadcasted_iota(jnp.int32, (tm, 1), 0)
    tail = carry_ref[...]
    prev1 = jnp.where(row == 0, tail[7:8], pltpu.roll(gated, 1, 0))
    prev2 = jnp.where(row == 0, tail[6:7], jnp.where(row == 1, tail[7:8], pltpu.roll(gated, 2, 0)))
    taps = cw_ref[...]
    y_conv = u_b * (taps[0:1] * prev2 + taps[1:2] * prev1 + taps[2:3] * gated)
    carry_ref[...] = gated[tm - 8:]
    yc_ref[0] = _rmsnorm(y_conv, cg_ref[...]).astype(BF16)

    a0 = 3 * cw
    cos_t, sin_t = cos_ref[0], sin_ref[0]
    q_t = _headnorm_rope_t(u[:, a0:a0 + ATT_WIDTH].T, qg_ref[...], cos_t, sin_t)
    k_t = _headnorm_rope_t(u[:, a0 + ATT_WIDTH:a0 + 2 * ATT_WIDTH].T, kg_ref[...], cos_t, sin_t)
    k = k_t.T
    qt_ref[0] = (q_t * Q_SCALE).astype(BF16)
    k_ref[0] = k.astype(BF16)
    km_ref[0, 0] = jnp.mean(k, axis=0, keepdims=True)
    vt_ref[0] = u[:, a0 + 2 * ATT_WIDTH:].T.astype(BF16)


def _mixer_in(x, mod_l, g1, w_in, conv_w, conv_g, q_g, k_g, cos_t, sin_t):
    bsz, seq, d = x.shape
    tm = TOKEN_TILE
    nb = seq // tm
    const2 = lambda b, t: (0, 0)
    transposed = pl.BlockSpec((1, ATT_WIDTH, tm), lambda b, t: (b, 0, t))
    return pl.pallas_call(
        _mixer_in_kernel,
        grid=(bsz, nb),
        in_specs=[
            pl.BlockSpec((1, tm, d), lambda b, t: (b, t, 0)),
            pl.BlockSpec((1, 6, d), lambda b, t: (b, 0, 0)),
            pl.BlockSpec((1, d), const2),
            pl.BlockSpec((d, IN_WIDTH), const2),
            pl.BlockSpec((CONV_K, CONV_WIDTH), const2),
            pl.BlockSpec((1, CONV_WIDTH), const2),
            pl.BlockSpec((HEAD_DIM, tm), const2),
            pl.BlockSpec((HEAD_DIM, tm), const2),
            pl.BlockSpec((1, HALF_DIM, tm), lambda b, t: (b, 0, t)),
            pl.BlockSpec((1, HALF_DIM, tm), lambda b, t: (b, 0, t)),
        ],
        out_specs=[
            pl.BlockSpec((1, tm, CONV_WIDTH), lambda b, t: (b, t, 0)),
            transposed,
            pl.BlockSpec((1, tm, ATT_WIDTH), lambda b, t: (b, t, 0)),
            transposed,
            pl.BlockSpec((1, 1, 1, ATT_WIDTH), lambda b, t: (b, t, 0, 0)),
        ],
        out_shape=[
            jax.ShapeDtypeStruct((bsz, seq, CONV_WIDTH), BF16),
            jax.ShapeDtypeStruct((bsz, ATT_WIDTH, seq), BF16),
            jax.ShapeDtypeStruct((bsz, seq, ATT_WIDTH), BF16),
            jax.ShapeDtypeStruct((bsz, ATT_WIDTH, seq), BF16),
            jax.ShapeDtypeStruct((bsz, nb, 1, ATT_WIDTH), F32),
        ],
        scratch_shapes=[pltpu.VMEM((8, CONV_WIDTH), F32)],
        compiler_params=_params(2),
        name="mixer_in",
    )(x, mod_l, g1, w_in, conv_w, conv_g, q_g, k_g, cos_t, sin_t)


def _block_bias(q_t, km, own, n_sel):
    tq = q_t.shape[-1]
    if own <= n_sel:
        return None
    km = km[:own]
    km_hi = km.astype(BF16).astype(F32)
    km_lo = km - km_hi
    first = lax.broadcasted_iota(jnp.int32, (1, HEAD_PAIR), 1) < HEAD_DIM
    km_rows = jnp.concatenate([jnp.where(first, km_hi, 0.0), jnp.where(first, 0.0, km_hi),
                               jnp.where(first, km_lo, 0.0), jnp.where(first, 0.0, km_lo)], axis=0)
    gate = jnp.dot(km_rows.astype(BF16), q_t, preferred_element_type=F32)
    blk = lax.broadcasted_iota(jnp.int32, (own, tq), 0)
    biases = []
    for h in range(2):
        g = gate[h * own:(h + 1) * own] + gate[(2 + h) * own:(3 + h) * own]
        rank = jnp.zeros((own, tq), F32)
        for m in range(own):
            gm = g[m:m + 1]
            ahead = (gm > g) | ((gm == g) & (blk > m))
            rank = rank + jnp.where(ahead, 1.0, 0.0)
        biases.append(jnp.where(rank < n_sel, 0.0, NEG))
    return biases


def _attend(own, n_sel, tq, qt_ref, k_ref, vt_ref, km_ref, o_ref):
    q_t = qt_ref[0, :, own * tq:(own + 1) * tq]
    dead = jnp.zeros((HEAD_DIM, tq), BF16)
    q_wide = jnp.concatenate([jnp.concatenate([q_t[:HEAD_DIM], dead], axis=0),
                              jnp.concatenate([dead, q_t[HEAD_DIM:]], axis=0)], axis=1)
    biases = _block_bias(q_t, km_ref[0], own, n_sel)
    if biases is not None:
        biases = jnp.concatenate(biases, axis=1)
    causal = (lax.broadcasted_iota(jnp.int32, (tq, tq), 0) <= lax.broadcasted_iota(jnp.int32, (tq, tq), 1))
    causal = jnp.concatenate([causal, causal], axis=1)
    n_keys = (own + 1) * tq
    m = None
    scores = []
    for n in range(own + 1):
        s = jnp.dot(k_ref[0, n * tq:(n + 1) * tq, :], q_wide, preferred_element_type=F32)
        if n == own:
            s = jnp.where(causal, s, NEG)
        elif biases is not None:
            s = s + biases[n:n + 1]
        scores.append(s)
        blk_max = jnp.max(s, axis=0, keepdims=True)
        m = blk_max if m is None else jnp.maximum(m, blk_max)
    l = jnp.zeros((1, 2 * tq), F32)
    probs = []
    for s in scores:
        p = jnp.exp2(s - m)
        l = l + jnp.sum(p, axis=0, keepdims=True)
        probs.append(p.astype(BF16))
    acc = jnp.dot(vt_ref[0, :, :n_keys], jnp.concatenate(probs, axis=0), preferred_element_type=F32)
    acc = acc * (1.0 / l)
    o_t = jnp.concatenate([acc[:HEAD_DIM, :tq], acc[HEAD_DIM:, tq:]], axis=0)
    o_ref[0, own * tq:(own + 1) * tq, :] = o_t.T.astype(BF16)


def _attn_kernel(qt_ref, k_ref, vt_ref, km_ref, o_ref):
    tq = MOBA_BLOCK
    nb = k_ref.shape[1] // tq
    n_sel = max(1, min(MOBA_TOPK, nb - 1))
    for own in range(nb):
        _attend(own, n_sel, tq, qt_ref, k_ref, vt_ref, km_ref, o_ref)


def _attention(q_t, k, v_t, kmean):
    bsz, seq, _ = k.shape
    tq = MOBA_BLOCK
    nb = seq // tq
    return pl.pallas_call(
        _attn_kernel,
        grid=(bsz, N_PAIRS),
        in_specs=[
            pl.BlockSpec((1, HEAD_PAIR, seq), lambda b, p: (b, p, 0)),
            pl.BlockSpec((1, seq, HEAD_PAIR), lambda b, p: (b, 0, p)),
            pl.BlockSpec((1, HEAD_PAIR, seq), lambda b, p: (b, p, 0)),
            pl.BlockSpec((1, nb, HEAD_PAIR), lambda b, p: (b, 0, p)),
        ],
        out_specs=pl.BlockSpec((1, seq, HEAD_PAIR), lambda b, p: (b, 0, p)),
        out_shape=jax.ShapeDtypeStruct((bsz, seq, ATT_WIDTH), BF16),
        compiler_params=_params(2),
        name="moba_attention",
    )(q_t, k, v_t, kmean)


def _mixer_out_kernel(x_ref, yc_ref, ya_ref, mod_ref, ag_ref, wout_ref, g2_ref, wr_ref, br_ref, tri_ref,
                      x1_ref, hs_ref, meta_ref, cnt_ref):
    @pl.when((pl.program_id(0) == 0) & (pl.program_id(1) == 0))
    def _():
        cnt_ref[...] = jnp.zeros_like(cnt_ref)

    mod = mod_ref[0]
    ya = _rmsnorm(ya_ref[0].astype(F32), ag_ref[...]).astype(BF16)
    proj = (jnp.dot(yc_ref[0], wout_ref[:CONV_WIDTH], preferred_element_type=F32)
            + jnp.dot(ya, wout_ref[CONV_WIDTH:], preferred_element_type=F32))
    x1 = x_ref[0] + mod[2:3] * proj
    x1_ref[0] = x1
    h2 = _rmsnorm(x1, g2_ref[...]) * (1.0 + mod[4:5]) + mod[3:4]

    logits = jnp.dot(h2.astype(BF16), wr_ref[...], preferred_element_type=F32) + br_ref[...]
    lane = lax.broadcasted_iota(jnp.int32, (1, LANES), 1).astype(F32)
    is_group = lane < N_GROUPS
    gl = jnp.where(is_group, logits, LOWEST)
    g_max = jnp.max(gl, axis=-1, keepdims=True)
    g_sel = jnp.min(jnp.where(gl == g_max, lane, float(LANES)), axis=-1, keepdims=True)
    p_group = 1.0 / jnp.sum(jnp.where(is_group, jnp.exp(gl - g_max), 0.0), axis=-1, keepdims=True)

    e0 = N_GROUPS + EXPERTS_PER_GROUP * g_sel
    el = jnp.where((lane >= e0) & (lane < e0 + EXPERTS_PER_GROUP), logits, LOWEST)
    v1 = jnp.max(el, axis=-1, keepdims=True)
    i1 = jnp.min(jnp.where(el == v1, lane, float(LANES)), axis=-1, keepdims=True)
    el2 = jnp.where(lane == i1, LOWEST, el)
    v2 = jnp.max(el2, axis=-1, keepdims=True)
    i2 = jnp.min(jnp.where(el2 == v2, lane, float(LANES)), axis=-1, keepdims=True)
    ex = jnp.exp(v2 - v1)
    w_top1 = p_group / (1.0 + ex)
    w_top2 = w_top1 * ex

    a, b = i1 - e0, i2 - e0
    lo, hi = jnp.minimum(a, b), jnp.maximum(a, b)
    bucket = g_sel * PAIRS_PER_GROUP + lo * (7.0 - lo) * 0.5 + (hi - lo - 1.0)
    w_lo = jnp.where(a < b, w_top1, w_top2)
    w_hi = jnp.where(a < b, w_top2, w_top1)

    onehot = jnp.where(lane == bucket, 1.0, 0.0)
    before = jnp.dot(tri_ref[...], onehot.astype(BF16), preferred_element_type=F32)
    cnt = cnt_ref[...]
    rank = jnp.sum((before + cnt) * onehot, axis=-1, keepdims=True)
    cnt_ref[...] = cnt + jnp.sum(onehot, axis=0, keepdims=True)

    meta_ref[0] = jnp.where(lane == 0.0, bucket, jnp.where(lane == 1.0, rank, 0.0))
    hs_ref[0, :, :D_MODEL] = h2
    hs_ref[0, :, D_MODEL:] = jnp.where(lane == 0.0, w_lo, jnp.where(lane == 1.0, w_hi, 0.0))


def _mixer_out(x, yc, ya, mod_l, attn_g, w_out, g2, w_router, b_router, tri):
    bsz, seq, d = x.shape
    tm = TOKEN_TILE
    const2 = lambda b, t: (0, 0)
    tok = lambda width: pl.BlockSpec((1, tm, width), lambda b, t: (b, t, 0))
    return pl.pallas_call(
        _mixer_out_kernel,
        grid=(bsz, seq // tm),
        in_specs=[
            tok(d), tok(CONV_WIDTH), tok(ATT_WIDTH),
            pl.BlockSpec((1, 6, d), lambda b, t: (b, 0, 0)),
            pl.BlockSpec((1, ATT_WIDTH), const2),
            pl.BlockSpec((d, d), const2),
            pl.BlockSpec((1, d), const2),
            pl.BlockSpec((d, LANES), const2),
            pl.BlockSpec((1, LANES), const2),
            pl.BlockSpec((tm, tm), const2),
        ],
        out_specs=[tok(d), tok(ROW_WORDS), tok(LANES), pl.BlockSpec((1, LANES), const2)],
        out_shape=[
            jax.ShapeDtypeStruct((bsz, seq, d), F32),
            jax.ShapeDtypeStruct((bsz, seq, ROW_WORDS), F32),
            jax.ShapeDtypeStruct((bsz, seq, LANES), F32),
            jax.ShapeDtypeStruct((1, LANES), F32),
        ],
        compiler_params=_params(2),
        name="mixer_out",
    )(x, yc, ya, mod_l, attn_g, w_out, g2, w_router, b_router, tri)


def _dispatch_kernel(pos_ref, src_ref, dst_in_ref, dst_ref, sem):
    del dst_in_ref
    rows = pos_ref.shape[-1]

    def row_copy(r):
        return pltpu.make_async_copy(src_ref.at[pl.ds(r, 1)], dst_ref.at[pl.ds(pos_ref[0, 0, r], 1)], sem)

    def issue(r, carry):
        row_copy(r).start()
        return carry

    def drain(r, carry):
        row_copy(r).wait()
        return carry

    lax.fori_loop(0, rows, issue, 0, unroll=ROW_UNROLL)
    lax.fori_loop(0, rows, drain, 0, unroll=ROW_UNROLL)


def _dispatch(pos, rows_tok, n_sorted):
    n, width = rows_tok.shape
    mt = MOVE_TILE
    return pl.pallas_call(
        _dispatch_kernel,
        grid=(n // mt,),
        in_specs=[
            pl.BlockSpec((1, 1, mt), lambda i: (i, 0, 0), memory_space=pltpu.SMEM),
            pl.BlockSpec((mt, width), lambda i: (i, 0)),
            pl.BlockSpec(memory_space=pl.ANY),
        ],
        out_specs=pl.BlockSpec(memory_space=pl.ANY),
        out_shape=jax.ShapeDtypeStruct((n_sorted, width), F32),
        scratch_shapes=[pltpu.SemaphoreType.DMA(())],
        input_output_aliases={2: 0},
        compiler_params=_params(1),
        name="moe_dispatch",
    )(pos.reshape(n // mt, 1, mt), rows_tok, jnp.zeros((n_sorted, width), F32))


def _expert_kernel(lo_ref, hi_ref, live_ref, hs_ref, wgu_lo, wgu_hi, wd_lo, wd_hi, y_ref):
    del lo_ref, hi_ref
    t = pl.program_id(0)

    @pl.when(live_ref[t] != 0)
    def _():
        hs = hs_ref[...]
        h = hs[:, :D_MODEL].astype(BF16)

        def expert(wgu_ref, wd_ref, w):
            gu = jnp.dot(h, wgu_ref[0], preferred_element_type=F32)
            hid = _silu(gu[:, :D_FF_EXPERT]) * gu[:, D_FF_EXPERT:] * w
            return jnp.dot(hid.astype(BF16), wd_ref[0], preferred_element_type=F32)

        y_ref[...] = (expert(wgu_lo, wd_lo, hs[:, D_MODEL:D_MODEL + 1])
                      + expert(wgu_hi, wd_hi, hs[:, D_MODEL + 1:D_MODEL + 2]))

    @pl.when(live_ref[t] == 0)
    def _():
        y_ref[...] = jnp.zeros_like(y_ref)


def _experts(tile_lo, tile_hi, tile_live, rows_sorted, w_gu, w_down):
    n_sorted = rows_sorted.shape[0]
    te = EXPERT_TILE
    grid_spec = pltpu.PrefetchScalarGridSpec(
        num_scalar_prefetch=3,
        grid=(n_sorted // te,),
        in_specs=[
            pl.BlockSpec((te, ROW_WORDS), lambda t, lo, hi, live: (t, 0)),
            pl.BlockSpec((1, D_MODEL, 2 * D_FF_EXPERT), lambda t, lo, hi, live: (lo[t], 0, 0)),
            pl.BlockSpec((1, D_MODEL, 2 * D_FF_EXPERT), lambda t, lo, hi, live: (hi[t], 0, 0)),
            pl.BlockSpec((1, D_FF_EXPERT, D_MODEL), lambda t, lo, hi, live: (lo[t], 0, 0)),
            pl.BlockSpec((1, D_FF_EXPERT, D_MODEL), lambda t, lo, hi, live: (hi[t], 0, 0)),
        ],
        out_specs=pl.BlockSpec((te, D_MODEL), lambda t, lo, hi, live: (t, 0)),
    )
    return pl.pallas_call(
        _expert_kernel,
        grid_spec=grid_spec,
        out_shape=jax.ShapeDtypeStruct((n_sorted, D_MODEL), F32),
        compiler_params=_params(1),
        name="moe_experts",
    )(tile_lo, tile_hi, tile_live, rows_sorted, w_gu, w_gu, w_down, w_down)


def _combine_kernel(pos_ref, x1_ref, mod_ref, ys_ref, o_ref, buf, sem):
    rows = pos_ref.shape[-1]

    def row_copy(r):
        return pltpu.make_async_copy(ys_ref.at[pl.ds(pos_ref[0, 0, r], 1)], buf.at[pl.ds(r, 1)], sem)

    def issue(r, carry):
        row_copy(r).start()
        return carry

    def drain(r, carry):
        row_copy(r).wait()
        return carry

    lax.fori_loop(0, rows, issue, 0, unroll=ROW_UNROLL)
    lax.fori_loop(0, rows, drain, 0, unroll=ROW_UNROLL)
    o_ref[0] = x1_ref[0] + mod_ref[0][5:6] * buf[...]


def _combine(pos, x1, mod_l, y_sorted):
    bsz, seq, d = x1.shape
    mt = MOVE_TILE
    per_seq = seq // mt
    return pl.pallas_call(
        _combine_kernel,
        grid=(bsz, per_seq),
        in_specs=[
            pl.BlockSpec((1, 1, mt), lambda b, t: (b * per_seq + t, 0, 0), memory_space=pltpu.SMEM),
            pl.BlockSpec((1, mt, d), lambda b, t: (b, t, 0)),
            pl.BlockSpec((1, 6, d), lambda b, t: (b, 0, 0)),
            pl.BlockSpec(memory_space=pl.ANY),
        ],
        out_specs=pl.BlockSpec((1, mt, d), lambda b, t: (b, t, 0)),
        out_shape=jax.ShapeDtypeStruct((bsz, seq, d), F32),
        scratch_shapes=[pltpu.VMEM((mt, d), F32), pltpu.SemaphoreType.DMA(())],
        compiler_params=_params(2),
        name="moe_combine",
    )(pos.reshape(bsz * per_seq, 1, mt), x1, mod_l, y_sorted)


def _routing_tables(meta, counts, n_tiles):
    te = EXPERT_TILE
    bucket = meta[:, 0].astype(jnp.int32)
    rank = meta[:, 1].astype(jnp.int32)
    counts = counts[0, :N_BUCKETS].astype(jnp.int32)
    tiles_per = (counts + te - 1) // te
    tile_end = jnp.cumsum(tiles_per)
    offsets = (tile_end - tiles_per) * te
    pos = offsets[bucket] + rank
    tile_ids = jnp.arange(n_tiles, dtype=jnp.int32)
    live = tile_ids < tile_end[-1]
    last_live = jnp.maximum(tile_end[-1] - 1, 0)
    tile_bucket = jnp.sum(jnp.minimum(tile_ids, last_live)[:, None] >= tile_end[None, :], axis=1, dtype=jnp.int32)
    tile_bucket = jnp.minimum(tile_bucket, N_BUCKETS - 1)
    pair_lo = jnp.array([0, 0, 0, 1, 1, 2], jnp.int32)
    pair_hi = jnp.array([1, 2, 3, 2, 3, 3], jnp.int32)
    group0 = (tile_bucket // PAIRS_PER_GROUP) * EXPERTS_PER_GROUP
    pair = tile_bucket % PAIRS_PER_GROUP
    return pos, group0 + pair_lo[pair], group0 + pair_hi[pair], live.astype(jnp.int32)


def kernel(x, c, positions, norm1_g, norm2_g, w_ada, b_ada, w_in, conv_w, q_norm_g, k_norm_g, conv_out_g,
           attn_out_g, w_out, w_router_group, b_router_group, w_router_expert, b_router_expert,
           w_gate, w_up, w_down):
    bsz, seq, d = x.shape
    depth = w_in.shape[0]
    n = bsz * seq
    tm = TOKEN_TILE
    n_tiles = n // EXPERT_TILE + N_BUCKETS
    n_sorted = n_tiles * EXPERT_TILE

    mod = _modulation(c, w_ada, b_ada).reshape(depth, bsz, 6, d)
    cos_t, sin_t = _rope_tables(positions)

    w_in_b = w_in.astype(BF16)
    w_out_b = w_out.astype(BF16)
    w_gu_b = jnp.concatenate([w_gate, w_up], axis=-1).astype(BF16)
    w_down_b = w_down.astype(BF16)
    pad = LANES - N_GROUPS - N_EXPERTS
    w_router = jnp.concatenate(
        [w_router_group, w_router_expert, jnp.zeros((depth, d, pad), F32)], axis=-1).astype(BF16)
    b_router = jnp.concatenate(
        [b_router_group, b_router_expert, jnp.zeros((depth, pad), F32)], axis=-1).reshape(depth, 1, LANES)
    q_gain = jnp.broadcast_to(q_norm_g[:, :, None], (depth, HEAD_DIM, tm))
    k_gain = jnp.broadcast_to(k_norm_g[:, :, None], (depth, HEAD_DIM, tm))
    idx = jnp.arange(tm)
    tri = (idx[None, :] < idx[:, None]).astype(BF16)

    for l in range(depth):
        yc, q_t, k, v_t, kmean = _mixer_in(
            x, mod[l], norm1_g[l][None], w_in_b[l], conv_w[l], conv_out_g[l][None],
            q_gain[l], k_gain[l], cos_t, sin_t)
        ya = _attention(q_t, k, v_t, kmean.reshape(bsz, seq // tm, ATT_WIDTH))
        x1, rows_tok, meta, counts = _mixer_out(
            x, yc, ya, mod[l], attn_out_g[l][None], w_out_b[l], norm2_g[l][None],
            w_router[l], b_router[l], tri)
        pos, tile_lo, tile_hi, tile_live = _routing_tables(meta.reshape(n, LANES), counts, n_tiles)
        rows_sorted = _dispatch(pos, rows_tok.reshape(n, ROW_WORDS), n_sorted)
        y_sorted = _experts(tile_lo, tile_hi, tile_live, rows_sorted, w_gu_b[l], w_down_b[l])
        x = _combine(pos, x1, mod[l], y_sorted)
    return x
```

```python
import functools

import jax
import jax.numpy as jnp
from jax import lax
from jax.experimental import pallas as pl
from jax.experimental.pallas import tpu as pltpu

F32 = jnp.float32
BF16 = jnp.bfloat16

D_MODEL = 1024
CONV_WIDTH = 512
CONV_K = 3
HEAD_DIM = 64
HALF_DIM = HEAD_DIM // 2
ATT_WIDTH = 512
ATT_HEADS = ATT_WIDTH // HEAD_DIM
IN_WIDTH = 3 * CONV_WIDTH + 3 * ATT_WIDTH
MOBA_BLOCK = 256
MOBA_TOPK = 3
ROPE_THETA = 10000.0
N_GROUPS = 4
EXPERTS_PER_GROUP = 4
N_EXPERTS = N_GROUPS * EXPERTS_PER_GROUP
D_FF_EXPERT = D_MODEL // 4
EPS = 1e-6
NEG = -1e30
LOWEST = -3e38
LOG2_E = 1.4426950408889634
Q_SCALE = HEAD_DIM ** -0.5 * LOG2_E

LANES = 128
LANE_GROUPS = D_MODEL // LANES
HEAD_PAIR = 2 * HEAD_DIM
N_PAIRS = ATT_WIDTH // HEAD_PAIR
TOKEN_TILE = MOBA_BLOCK
PAIRS_PER_GROUP = 6
N_BUCKETS = N_GROUPS * PAIRS_PER_GROUP
EXPERT_TILE = 256
ROW_WORDS = D_MODEL + LANES
MOVE_TILE = 512
ROW_UNROLL = 8
VMEM_LIMIT = 48 * 1024 * 1024


def _params(n_axes, vmem=VMEM_LIMIT):
    return pltpu.CompilerParams(dimension_semantics=("arbitrary",) * n_axes, vmem_limit_bytes=vmem)


def _rmsnorm(x, g):
    return x * lax.rsqrt(jnp.mean(x * x, axis=-1, keepdims=True) + EPS) * g


def _silu(x):
    return x * (1.0 / (1.0 + jnp.exp(-x)))


def _mod_kernel(c_ref, w_ref, b_ref, o_ref):
    ca = _silu(c_ref[...]).astype(BF16)
    o_ref[0] = jnp.dot(ca, w_ref[0].astype(BF16), preferred_element_type=F32) + b_ref[0]


def _modulation(c, w_ada, b_ada):
    depth, d, width = w_ada.shape
    bsz = c.shape[0]
    tn = 1536
    return pl.pallas_call(
        _mod_kernel,
        grid=(depth, width // tn),
        in_specs=[
            pl.BlockSpec((bsz, d), lambda l, j: (0, 0)),
            pl.BlockSpec((1, d, tn), lambda l, j: (l, 0, j)),
            pl.BlockSpec((1, 1, tn), lambda l, j: (l, 0, j)),
        ],
        out_specs=pl.BlockSpec((1, bsz, tn), lambda l, j: (l, 0, j)),
        out_shape=jax.ShapeDtypeStruct((depth, bsz, width), F32),
        compiler_params=_params(2),
        name="modulation",
    )(c, w_ada, b_ada.reshape(depth, 1, width))


def _rope_kernel(pos_ref, freq_ref, cos_ref, sin_ref):
    ts = pos_ref.shape[-1]
    freq = jnp.concatenate([freq_ref[...]] * (ts // LANES), axis=1)
    ang = pos_ref[0] * freq
    cos_ref[0] = jnp.cos(ang)
    sin_ref[0] = jnp.sin(ang)


def _rope_tables(positions):
    bsz, seq = positions.shape
    ts = 512
    inv_freq = ROPE_THETA ** (-jnp.arange(0, HEAD_DIM, 2, dtype=F32) / HEAD_DIM)
    freq = jnp.broadcast_to(inv_freq[:, None], (HALF_DIM, LANES))
    pos = positions.astype(F32).reshape(bsz, 1, seq)
    spec = pl.BlockSpec((1, HALF_DIM, ts), lambda b, t: (b, 0, t))
    return pl.pallas_call(
        _rope_kernel,
        grid=(bsz, seq // ts),
        in_specs=[pl.BlockSpec((1, 1, ts), lambda b, t: (b, 0, t)),
                  pl.BlockSpec((HALF_DIM, LANES), lambda b, t: (0, 0))],
        out_specs=[spec, spec],
        out_shape=[jax.ShapeDtypeStruct((bsz, HALF_DIM, seq), F32)] * 2,
        compiler_params=_params(2),
        name="rope_tables",
    )(pos, freq)


def _headnorm_rope_t(z_t, gain, cos_t, sin_t):
    outs = []
    for hd in range(ATT_HEADS):
        z = z_t[hd * HEAD_DIM:(hd + 1) * HEAD_DIM]
        zn = z * lax.rsqrt(jnp.mean(z * z, axis=0, keepdims=True) + EPS) * gain
        x1, x2 = zn[:HALF_DIM], zn[HALF_DIM:]
        outs.append(x1 * cos_t - x2 * sin_t)
        outs.append(x2 * cos_t + x1 * sin_t)
    return jnp.concatenate(outs, axis=0)


def _mixer_in_kernel(x_ref, mod_ref, g1_ref, win_ref, cw_ref, cg_ref, qg_ref, kg_ref, cos_ref, sin_ref,
                     yc_ref, qt_ref, k_ref, vt_ref, km_ref, carry_ref):
    tm = x_ref.shape[1]

    @pl.when(pl.program_id(1) == 0)
    def _():
        carry_ref[...] = jnp.zeros_like(carry_ref)

    mod = mod_ref[0]
    h = _rmsnorm(x_ref[0], g1_ref[...]) * (1.0 + mod[1:2]) + mod[0:1]
    u = jnp.dot(h.astype(BF16), win_ref[...], preferred_element_type=F32)

    cw = CONV_WIDTH
    u_b, gated = u[:, :cw], u[:, cw:2 * cw] * u[:, 2 * cw:3 * cw]
    row = lax.broadcasted_iota(jnp.int32, (tm, 1), 0)
    tail = carry_ref[...]
    prev1 = jnp.where(row == 0, tail[7:8], pltpu.roll(gated, 1, 0))
    prev2 = jnp.where(row == 0, tail[6:7], jnp.where(row == 1, tail[7:8], pltpu.roll(gated, 2, 0)))
    taps = cw_ref[...]
    y_conv = u_b * (taps[0:1] * prev2 + taps[1:2] * prev1 + taps[2:3] * gated)
    carry_ref[...] = gated[tm - 8:]
    yc_ref[0] = _rmsnorm(y_conv, cg_ref[...]).astype(BF16)

    a0 = 3 * cw
    cos_t, sin_t = cos_ref[0], sin_ref[0]
    q_t = _headnorm_rope_t(u[:, a0:a0 + ATT_WIDTH].T, qg_ref[...], cos_t, sin_t)
    k_t = _headnorm_rope_t(u[:, a0 + ATT_WIDTH:a0 + 2 * ATT_WIDTH].T, kg_ref[...], cos_t, sin_t)
    k = k_t.T
    qt_ref[0] = (q_t * Q_SCALE).astype(BF16)
    k_ref[0] = k.astype(BF16)
    km_ref[0, 0] = jnp.mean(k, axis=0, keepdims=True)
    vt_ref[0] = u[:, a0 + 2 * ATT_WIDTH:].T.astype(BF16)


def _mixer_in(x, mod_l, g1, w_in, conv_w, conv_g, q_g, k_g, cos_t, sin_t):
    bsz, seq, d = x.shape
    tm = TOKEN_TILE
    nb = seq // tm
    const2 = lambda b, t: (0, 0)
    transposed = pl.BlockSpec((1, ATT_WIDTH, tm), lambda b, t: (b, 0, t))
    return pl.pallas_call(
        _mixer_in_kernel,
        grid=(bsz, nb),
        in_specs=[
            pl.BlockSpec((1, tm, d), lambda b, t: (b, t, 0)),
            pl.BlockSpec((1, 6, d), lambda b, t: (b, 0, 0)),
            pl.BlockSpec((1, d), const2),
            pl.BlockSpec((d, IN_WIDTH), const2),
            pl.BlockSpec((CONV_K, CONV_WIDTH), const2),
            pl.BlockSpec((1, CONV_WIDTH), const2),
            pl.BlockSpec((HEAD_DIM, tm), const2),
            pl.BlockSpec((HEAD_DIM, tm), const2),
            pl.BlockSpec((1, HALF_DIM, tm), lambda b, t: (b, 0, t)),
            pl.BlockSpec((1, HALF_DIM, tm), lambda b, t: (b, 0, t)),
        ],
        out_specs=[
            pl.BlockSpec((1, tm, CONV_WIDTH), lambda b, t: (b, t, 0)),
            transposed,
            pl.BlockSpec((1, tm, ATT_WIDTH), lambda b, t: (b, t, 0)),
            transposed,
            pl.BlockSpec((1, 1, 1, ATT_WIDTH), lambda b, t: (b, t, 0, 0)),
        ],
        out_shape=[
            jax.ShapeDtypeStruct((bsz, seq, CONV_WIDTH), BF16),
            jax.ShapeDtypeStruct((bsz, ATT_WIDTH, seq), BF16),
            jax.ShapeDtypeStruct((bsz, seq, ATT_WIDTH), BF16),
            jax.ShapeDtypeStruct((bsz, ATT_WIDTH, seq), BF16),
            jax.ShapeDtypeStruct((bsz, nb, 1, ATT_WIDTH), F32),
        ],
        scratch_shapes=[pltpu.VMEM((8, CONV_WIDTH), F32)],
        compiler_params=_params(2),
        name="mixer_in",
    )(x, mod_l, g1, w_in, conv_w, conv_g, q_g, k_g, cos_t, sin_t)


def _block_bias(q_t, km, own, n_sel):
    tq = q_t.shape[-1]
    if own <= n_sel:
        return None
    km = km[:own]
    km_hi = km.astype(BF16).astype(F32)
    km_lo = km - km_hi
    first = lax.broadcasted_iota(jnp.int32, (1, HEAD_PAIR), 1) < HEAD_DIM
    km_rows = jnp.concatenate([jnp.where(first, km_hi, 0.0), jnp.where(first, 0.0, km_hi),
                               jnp.where(first, km_lo, 0.0), jnp.where(first, 0.0, km_lo)], axis=0)
    gate = jnp.dot(km_rows.astype(BF16), q_t, preferred_element_type=F32)
    blk = lax.broadcasted_iota(jnp.int32, (own, tq), 0)
    biases = []
    for h in range(2):
        g = gate[h * own:(h + 1) * own] + gate[(2 + h) * own:(3 + h) * own]
        rank = jnp.zeros((own, tq), F32)
        for m in range(own):
            gm = g[m:m + 1]
            ahead = (gm > g) | ((gm == g) & (blk > m))
            rank = rank + jnp.where(ahead, 1.0, 0.0)
        biases.append(jnp.where(rank < n_sel, 0.0, NEG))
    return biases


def _attend(own, n_sel, tq, qt_ref, k_ref, vt_ref, km_ref, o_ref):
    q_t = qt_ref[0, :, own * tq:(own + 1) * tq]
    dead = jnp.zeros((HEAD_DIM, tq), BF16)
    q_wide = jnp.concatenate([jnp.concatenate([q_t[:HEAD_DIM], dead], axis=0),
                              jnp.concatenate([dead, q_t[HEAD_DIM:]], axis=0)], axis=1)
    biases = _block_bias(q_t, km_ref[0], own, n_sel)
    if biases is not None:
        biases = jnp.concatenate(biases, axis=1)
    causal = (lax.broadcasted_iota(jnp.int32, (tq, tq), 0) <= lax.broadcasted_iota(jnp.int32, (tq, tq), 1))
    causal = jnp.concatenate([causal, causal], axis=1)
    n_keys = (own + 1) * tq
    m = None
    scores = []
    for n in range(own + 1):
        s = jnp.dot(k_ref[0, n * tq:(n + 1) * tq, :], q_wide, preferred_element_type=F32)
        if n == own:
            s = jnp.where(causal, s, NEG)
        elif biases is not None:
            s = s + biases[n:n + 1]
        scores.append(s)
        blk_max = jnp.max(s, axis=0, keepdims=True)
        m = blk_max if m is None else jnp.maximum(m, blk_max)
    l = jnp.zeros((1, 2 * tq), F32)
    probs = []
    for s in scores:
        p = jnp.exp2(s - m)
        l = l + jnp.sum(p, axis=0, keepdims=True)
        probs.append(p.astype(BF16))
    acc = jnp.dot(vt_ref[0, :, :n_keys], jnp.concatenate(probs, axis=0), preferred_element_type=F32)
    acc = acc * (1.0 / l)
    o_t = jnp.concatenate([acc[:HEAD_DIM, :tq], acc[HEAD_DIM:, tq:]], axis=0)
    o_ref[0, own * tq:(own + 1) * tq, :] = o_t.T.astype(BF16)


def _attn_kernel(qt_ref, k_ref, vt_ref, km_ref, o_ref):
    tq = MOBA_BLOCK
    nb = k_ref.shape[1] // tq
    n_sel = max(1, min(MOBA_TOPK, nb - 1))
    for own in range(nb):
        _attend(own, n_sel, tq, qt_ref, k_ref, vt_ref, km_ref, o_ref)


def _attention(q_t, k, v_t, kmean):
    bsz, seq, _ = k.shape
    tq = MOBA_BLOCK
    nb = seq // tq
    return pl.pallas_call(
        _attn_kernel,
        grid=(bsz, N_PAIRS),
        in_specs=[
            pl.BlockSpec((1, HEAD_PAIR, seq), lambda b, p: (b, p, 0)),
            pl.BlockSpec((1, seq, HEAD_PAIR), lambda b, p: (b, 0, p)),
            pl.BlockSpec((1, HEAD_PAIR, seq), lambda b, p: (b, p, 0)),
            pl.BlockSpec((1, nb, HEAD_PAIR), lambda b, p: (b, 0, p)),
        ],
        out_specs=pl.BlockSpec((1, seq, HEAD_PAIR), lambda b, p: (b, 0, p)),
        out_shape=jax.ShapeDtypeStruct((bsz, seq, ATT_WIDTH), BF16),
        compiler_params=_params(2),
        name="moba_attention",
    )(q_t, k, v_t, kmean)


def _mixer_out_kernel(x_ref, yc_ref, ya_ref, mod_ref, ag_ref, wout_ref, g2_ref, wr_ref, br_ref, tri_ref,
                      x1_ref, hs_ref, meta_ref, cnt_ref):
    @pl.when((pl.program_id(0) == 0) & (pl.program_id(1) == 0))
    def _():
        cnt_ref[...] = jnp.zeros_like(cnt_ref)

    mod = mod_ref[0]
    ya = _rmsnorm(ya_ref[0].astype(F32), ag_ref[...]).astype(BF16)
    proj = (jnp.dot(yc_ref[0], wout_ref[:CONV_WIDTH], preferred_element_type=F32)
            + jnp.dot(ya, wout_ref[CONV_WIDTH:], preferred_element_type=F32))
    x1 = x_ref[0] + mod[2:3] * proj
    x1_ref[0] = x1
    h2 = _rmsnorm(x1, g2_ref[...]) * (1.0 + mod[4:5]) + mod[3:4]

    logits = jnp.dot(h2.astype(BF16), wr_ref[...], preferred_element_type=F32) + br_ref[...]
    lane = lax.broadcasted_iota(jnp.int32, (1, LANES), 1).astype(F32)
    is_group = lane < N_GROUPS
    gl = jnp.where(is_group, logits, LOWEST)
    g_max = jnp.max(gl, axis=-1, keepdims=True)
    g_sel = jnp.min(jnp.where(gl == g_max, lane, float(LANES)), axis=-1, keepdims=True)
    p_group = 1.0 / jnp.sum(jnp.where(is_group, jnp.exp(gl - g_max), 0.0), axis=-1, keepdims=True)

    e0 = N_GROUPS + EXPERTS_PER_GROUP * g_sel
    el = jnp.where((lane >= e0) & (lane < e0 + EXPERTS_PER_GROUP), logits, LOWEST)
    v1 = jnp.max(el, axis=-1, keepdims=True)
    i1 = jnp.min(jnp.where(el == v1, lane, float(LANES)), axis=-1, keepdims=True)
    el2 = jnp.where(lane == i1, LOWEST, el)
    v2 = jnp.max(el2, axis=-1, keepdims=True)
    i2 = jnp.min(jnp.where(el2 == v2, lane, float(LANES)), axis=-1, keepdims=True)
    ex = jnp.exp(v2 - v1)
    w_top1 = p_group / (1.0 + ex)
    w_top2 = w_top1 * ex

    a, b = i1 - e0, i2 - e0
    lo, hi = jnp.minimum(a, b), jnp.maximum(a, b)
    bucket = g_sel * PAIRS_PER_GROUP + lo * (7.0 - lo) * 0.5 + (hi - lo - 1.0)
    w_lo = jnp.where(a < b, w_top1, w_top2)
    w_hi = jnp.where(a < b, w_top2, w_top1)

    onehot = jnp.where(lane == bucket, 1.0, 0.0)
    before = jnp.dot(tri_ref[...], onehot.astype(BF16), preferred_element_type=F32)
    cnt = cnt_ref[...]
    rank = jnp.sum((before + cnt) * onehot, axis=-1, keepdims=True)
    cnt_ref[...] = cnt + jnp.sum(onehot, axis=0, keepdims=True)

    meta_ref[0] = jnp.where(lane == 0.0, bucket, jnp.where(lane == 1.0, rank, 0.0))
    hs_ref[0, :, :D_MODEL] = h2
    hs_ref[0, :, D_MODEL:] = jnp.where(lane == 0.0, w_lo, jnp.where(lane == 1.0, w_hi, 0.0))


def _mixer_out(x, yc, ya, mod_l, attn_g, w_out, g2, w_router, b_router, tri):
    bsz, seq, d = x.shape
    tm = TOKEN_TILE
    const2 = lambda b, t: (0, 0)
    tok = lambda width: pl.BlockSpec((1, tm, width), lambda b, t: (b, t, 0))
    return pl.pallas_call(
        _mixer_out_kernel,
        grid=(bsz, seq // tm),
        in_specs=[
            tok(d), tok(CONV_WIDTH), tok(ATT_WIDTH),
            pl.BlockSpec((1, 6, d), lambda b, t: (b, 0, 0)),
            pl.BlockSpec((1, ATT_WIDTH), const2),
            pl.BlockSpec((d, d), const2),
            pl.BlockSpec((1, d), const2),
            pl.BlockSpec((d, LANES), const2),
            pl.BlockSpec((1, LANES), const2),
            pl.BlockSpec((tm, tm), const2),
        ],
        out_specs=[tok(d), tok(ROW_WORDS), tok(LANES), pl.BlockSpec((1, LANES), const2)],
        out_shape=[
            jax.ShapeDtypeStruct((bsz, seq, d), F32),
            jax.ShapeDtypeStruct((bsz, seq, ROW_WORDS), F32),
            jax.ShapeDtypeStruct((bsz, seq, LANES), F32),
            jax.ShapeDtypeStruct((1, LANES), F32),
        ],
        compiler_params=_params(2),
        name="mixer_out",
    )(x, yc, ya, mod_l, attn_g, w_out, g2, w_router, b_router, tri)


def _move_rows(rows, row_copy):
    def trips(method):
        def body(g, carry):
            base = pl.multiple_of(g * ROW_UNROLL, ROW_UNROLL)
            for j in range(ROW_UNROLL):
                getattr(row_copy(base, j), method)()
            return carry
        lax.fori_loop(0, rows // ROW_UNROLL, body, 0)

    trips("start")
    trips("wait")


def _dispatch_kernel(pos_ref, src_ref, dst_in_ref, dst_ref, sem):
    del dst_in_ref

    def row_copy(base, j):
        return pltpu.make_async_copy(src_ref.at[pl.ds(base + j, 1)],
                                     dst_ref.at[pl.ds(pos_ref[0, 0, base + j], 1)], sem)

    _move_rows(pos_ref.shape[-1], row_copy)


def _dispatch(pos, rows_tok, sorted_init):
    n, width = rows_tok.shape
    n_sorted = sorted_init.shape[0]
    mt = MOVE_TILE
    return pl.pallas_call(
        _dispatch_kernel,
        grid=(n // mt,),
        in_specs=[
            pl.BlockSpec((1, 1, mt), lambda i: (i, 0, 0), memory_space=pltpu.SMEM),
            pl.BlockSpec((mt, width), lambda i: (i, 0)),
            pl.BlockSpec(memory_space=pl.ANY),
        ],
        out_specs=pl.BlockSpec(memory_space=pl.ANY),
        out_shape=jax.ShapeDtypeStruct((n_sorted, width), F32),
        scratch_shapes=[pltpu.SemaphoreType.DMA(())],
        input_output_aliases={2: 0},
        compiler_params=_params(1),
        name="moe_dispatch",
    )(pos.reshape(n // mt, 1, mt), rows_tok, sorted_init)


def _expert_kernel(lo_ref, hi_ref, live_ref, hs_ref, wgu_lo, wgu_hi, wd_lo, wd_hi, y_ref):
    del lo_ref, hi_ref
    t = pl.program_id(0)

    @pl.when(live_ref[t] != 0)
    def _():
        hs = hs_ref[...]
        h = hs[:, :D_MODEL].astype(BF16)

        def expert(wgu_ref, wd_ref, w):
            gu = jnp.dot(h, wgu_ref[0], preferred_element_type=F32)
            hid = _silu(gu[:, :D_FF_EXPERT]) * gu[:, D_FF_EXPERT:] * w
            return jnp.dot(hid.astype(BF16), wd_ref[0], preferred_element_type=F32)

        y = (expert(wgu_lo, wd_lo, hs[:, D_MODEL:D_MODEL + 1])
             + expert(wgu_hi, wd_hi, hs[:, D_MODEL + 1:D_MODEL + 2]))
        for c in range(LANE_GROUPS):
            y_ref[pl.ds(c, y.shape[0], stride=LANE_GROUPS), :] = y[:, c * LANES:(c + 1) * LANES]

    @pl.when(live_ref[t] == 0)
    def _():
        y_ref[...] = jnp.zeros_like(y_ref)


def _experts(tile_lo, tile_hi, tile_live, rows_sorted, w_gu, w_down):
    n_sorted = rows_sorted.shape[0]
    te = EXPERT_TILE
    grid_spec = pltpu.PrefetchScalarGridSpec(
        num_scalar_prefetch=3,
        grid=(n_sorted // te,),
        in_specs=[
            pl.BlockSpec((te, ROW_WORDS), lambda t, lo, hi, live: (t, 0)),
            pl.BlockSpec((1, D_MODEL, 2 * D_FF_EXPERT), lambda t, lo, hi, live: (lo[t], 0, 0)),
            pl.BlockSpec((1, D_MODEL, 2 * D_FF_EXPERT), lambda t, lo, hi, live: (hi[t], 0, 0)),
            pl.BlockSpec((1, D_FF_EXPERT, D_MODEL), lambda t, lo, hi, live: (lo[t], 0, 0)),
            pl.BlockSpec((1, D_FF_EXPERT, D_MODEL), lambda t, lo, hi, live: (hi[t], 0, 0)),
        ],
        out_specs=pl.BlockSpec((te * LANE_GROUPS, LANES), lambda t, lo, hi, live: (t, 0)),
    )
    return pl.pallas_call(
        _expert_kernel,
        grid_spec=grid_spec,
        out_shape=jax.ShapeDtypeStruct((n_sorted * LANE_GROUPS, LANES), F32),
        compiler_params=_params(1),
        name="moe_experts",
    )(tile_lo, tile_hi, tile_live, rows_sorted, w_gu, w_gu, w_down, w_down)


def _combine_kernel(pos_ref, x1_ref, mod_ref, ys_ref, o_ref, buf, sem):
    rows = pos_ref.shape[-1]
    g = LANE_GROUPS

    def row_copy(base, j):
        src = pl.multiple_of(pos_ref[0, 0, base + j] * g, g)
        return pltpu.make_async_copy(ys_ref.at[pl.ds(src, g)], buf.at[pl.ds((base + j) * g, g)], sem)

    _move_rows(rows, row_copy)
    y = jnp.concatenate([buf[pl.ds(c, rows, stride=g), :] for c in range(g)], axis=1)
    o_ref[0] = x1_ref[0] + mod_ref[0][5:6] * y


def _combine(pos, x1, mod_l, y_sorted):
    bsz, seq, d = x1.shape
    mt = MOVE_TILE
    per_seq = seq // mt
    return pl.pallas_call(
        _combine_kernel,
        grid=(bsz, per_seq),
        in_specs=[
            pl.BlockSpec((1, 1, mt), lambda b, t: (b * per_seq + t, 0, 0), memory_space=pltpu.SMEM),
            pl.BlockSpec((1, mt, d), lambda b, t: (b, t, 0)),
            pl.BlockSpec((1, 6, d), lambda b, t: (b, 0, 0)),
            pl.BlockSpec(memory_space=pl.ANY),
        ],
        out_specs=pl.BlockSpec((1, mt, d), lambda b, t: (b, t, 0)),
        out_shape=jax.ShapeDtypeStruct((bsz, seq, d), F32),
        scratch_shapes=[pltpu.VMEM((mt * LANE_GROUPS, LANES), F32), pltpu.SemaphoreType.DMA(())],
        compiler_params=_params(2),
        name="moe_combine",
    )(pos.reshape(bsz * per_seq, 1, mt), x1, mod_l, y_sorted)


def _routing_tables(meta, counts, n_tiles):
    te = EXPERT_TILE
    bucket = meta[:, 0].astype(jnp.int32)
    rank = meta[:, 1].astype(jnp.int32)
    counts = counts[0, :N_BUCKETS].astype(jnp.int32)
    tiles_per = (counts + te - 1) // te
    tile_end = jnp.cumsum(tiles_per)
    offsets = (tile_end - tiles_per) * te
    pos = offsets[bucket] + rank
    tile_ids = jnp.arange(n_tiles, dtype=jnp.int32)
    live = tile_ids < tile_end[-1]
    last_live = jnp.maximum(tile_end[-1] - 1, 0)
    tile_bucket = jnp.sum(jnp.minimum(tile_ids, last_live)[:, None] >= tile_end[None, :], axis=1, dtype=jnp.int32)
    tile_bucket = jnp.minimum(tile_bucket, N_BUCKETS - 1)
    pair_lo = jnp.array([0, 0, 0, 1, 1, 2], jnp.int32)
    pair_hi = jnp.array([1, 2, 3, 2, 3, 3], jnp.int32)
    group0 = (tile_bucket // PAIRS_PER_GROUP) * EXPERTS_PER_GROUP
    pair = tile_bucket % PAIRS_PER_GROUP
    return pos, group0 + pair_lo[pair], group0 + pair_hi[pair], live.astype(jnp.int32)


def kernel(x, c, positions, norm1_g, norm2_g, w_ada, b_ada, w_in, conv_w, q_norm_g, k_norm_g, conv_out_g,
           attn_out_g, w_out, w_router_group, b_router_group, w_router_expert, b_router_expert,
           w_gate, w_up, w_down):
    bsz, seq, d = x.shape
    depth = w_in.shape[0]
    n = bsz * seq
    tm = TOKEN_TILE
    n_tiles = n // EXPERT_TILE + N_BUCKETS
    n_sorted = n_tiles * EXPERT_TILE

    mod = _modulation(c, w_ada, b_ada).reshape(depth, bsz, 6, d)
    cos_t, sin_t = _rope_tables(positions)

    w_in_b = w_in.astype(BF16)
    w_out_b = w_out.astype(BF16)
    w_gu_b = jnp.concatenate([w_gate, w_up], axis=-1).astype(BF16)
    w_down_b = w_down.astype(BF16)
    pad = LANES - N_GROUPS - N_EXPERTS
    w_router = jnp.concatenate(
        [w_router_group, w_router_expert, jnp.zeros((depth, d, pad), F32)], axis=-1).astype(BF16)
    b_router = jnp.concatenate(
        [b_router_group, b_router_expert, jnp.zeros((depth, pad), F32)], axis=-1).reshape(depth, 1, LANES)
    q_gain = jnp.broadcast_to(q_norm_g[:, :, None], (depth, HEAD_DIM, tm))
    k_gain = jnp.broadcast_to(k_norm_g[:, :, None], (depth, HEAD_DIM, tm))
    idx = jnp.arange(tm)
    tri = (idx[None, :] < idx[:, None]).astype(BF16)

    rows_sorted = jnp.zeros((n_sorted, ROW_WORDS), F32)
    for l in range(depth):
        yc, q_t, k, v_t, kmean = _mixer_in(
            x, mod[l], norm1_g[l][None], w_in_b[l], conv_w[l], conv_out_g[l][None],
            q_gain[l], k_gain[l], cos_t, sin_t)
        ya = _attention(q_t, k, v_t, kmean.reshape(bsz, seq // tm, ATT_WIDTH))
        x1, rows_tok, meta, counts = _mixer_out(
            x, yc, ya, mod[l], attn_out_g[l][None], w_out_b[l], norm2_g[l][None],
            w_router[l], b_router[l], tri)
        pos, tile_lo, tile_hi, tile_live = _routing_tables(meta.reshape(n, LANES), counts, n_tiles)
        rows_sorted = _dispatch(pos, rows_tok.reshape(n, ROW_WORDS), rows_sorted)
        y_sorted = _experts(tile_lo, tile_hi, tile_live, rows_sorted, w_gu_b[l], w_down_b[l])
        x = _combine(pos, x1, mod[l], y_sorted)
    return x
```

```python
import functools

import jax
import jax.numpy as jnp
from jax import lax
from jax.experimental import pallas as pl
from jax.experimental.pallas import tpu as pltpu

F32 = jnp.float32
BF16 = jnp.bfloat16

D_MODEL = 1024
CONV_WIDTH = 512
CONV_K = 3
HEAD_DIM = 64
HALF_DIM = HEAD_DIM // 2
ATT_WIDTH = 512
ATT_HEADS = ATT_WIDTH // HEAD_DIM
IN_WIDTH = 3 * CONV_WIDTH + 3 * ATT_WIDTH
MOBA_BLOCK = 256
MOBA_TOPK = 3
ROPE_THETA = 10000.0
N_GROUPS = 4
EXPERTS_PER_GROUP = 4
N_EXPERTS = N_GROUPS * EXPERTS_PER_GROUP
D_FF_EXPERT = D_MODEL // 4
EPS = 1e-6
NEG = -1e30
LOWEST = -3e38
LOG2_E = 1.4426950408889634
Q_SCALE = HEAD_DIM ** -0.5 * LOG2_E

LANES = 128
LANE_GROUPS = D_MODEL // LANES
HEAD_PAIR = 2 * HEAD_DIM
N_PAIRS = ATT_WIDTH // HEAD_PAIR
TOKEN_TILE = MOBA_BLOCK
PAIRS_PER_GROUP = 6
N_BUCKETS = N_GROUPS * PAIRS_PER_GROUP
EXPERT_TILE = 256
ROW_WORDS = D_MODEL + LANES
MOVE_TILE = 512
ROW_UNROLL = 8
VMEM_LIMIT = 48 * 1024 * 1024


def _params(n_axes, vmem=VMEM_LIMIT):
    return pltpu.CompilerParams(dimension_semantics=("arbitrary",) * n_axes, vmem_limit_bytes=vmem)


def _rmsnorm(x, g):
    return x * lax.rsqrt(jnp.mean(x * x, axis=-1, keepdims=True) + EPS) * g


def _silu(x):
    return x * (1.0 / (1.0 + jnp.exp(-x)))


def _for_rows(rows, row_copy, method):
    def body(g, carry):
        base = pl.multiple_of(g * ROW_UNROLL, ROW_UNROLL)
        for j in range(ROW_UNROLL):
            getattr(row_copy(base, j), method)()
        return carry

    lax.fori_loop(0, rows // ROW_UNROLL, body, 0)


def _mod_kernel(c_ref, w_ref, b_ref, o_ref):
    ca = _silu(c_ref[...]).astype(BF16)
    o_ref[0] = jnp.dot(ca, w_ref[0].astype(BF16), preferred_element_type=F32) + b_ref[0]


def _modulation(c, w_ada, b_ada):
    depth, d, width = w_ada.shape
    bsz = c.shape[0]
    tn = 1536
    return pl.pallas_call(
        _mod_kernel,
        grid=(depth, width // tn),
        in_specs=[
            pl.BlockSpec((bsz, d), lambda l, j: (0, 0)),
            pl.BlockSpec((1, d, tn), lambda l, j: (l, 0, j)),
            pl.BlockSpec((1, 1, tn), lambda l, j: (l, 0, j)),
        ],
        out_specs=pl.BlockSpec((1, bsz, tn), lambda l, j: (l, 0, j)),
        out_shape=jax.ShapeDtypeStruct((depth, bsz, width), F32),
        compiler_params=_params(2),
        name="modulation",
    )(c, w_ada, b_ada.reshape(depth, 1, width))


def _rope_kernel(pos_ref, freq_ref, cos_ref, sin_ref):
    ts = pos_ref.shape[-1]
    freq = jnp.concatenate([freq_ref[...]] * (ts // LANES), axis=1)
    ang = pos_ref[0] * freq
    cos_ref[0] = jnp.cos(ang)
    sin_ref[0] = jnp.sin(ang)


def _rope_tables(positions):
    bsz, seq = positions.shape
    ts = 512
    inv_freq = ROPE_THETA ** (-jnp.arange(0, HEAD_DIM, 2, dtype=F32) / HEAD_DIM)
    freq = jnp.broadcast_to(inv_freq[:, None], (HALF_DIM, LANES))
    pos = positions.astype(F32).reshape(bsz, 1, seq)
    spec = pl.BlockSpec((1, HALF_DIM, ts), lambda b, t: (b, 0, t))
    return pl.pallas_call(
        _rope_kernel,
        grid=(bsz, seq // ts),
        in_specs=[pl.BlockSpec((1, 1, ts), lambda b, t: (b, 0, t)),
                  pl.BlockSpec((HALF_DIM, LANES), lambda b, t: (0, 0))],
        out_specs=[spec, spec],
        out_shape=[jax.ShapeDtypeStruct((bsz, HALF_DIM, seq), F32)] * 2,
        compiler_params=_params(2),
        name="rope_tables",
    )(pos, freq)


def _headnorm_rope_t(z_t, gain, cos_t, sin_t):
    outs = []
    for hd in range(ATT_HEADS):
        z = z_t[hd * HEAD_DIM:(hd + 1) * HEAD_DIM]
        zn = z * lax.rsqrt(jnp.mean(z * z, axis=0, keepdims=True) + EPS) * gain
        x1, x2 = zn[:HALF_DIM], zn[HALF_DIM:]
        outs.append(x1 * cos_t - x2 * sin_t)
        outs.append(x2 * cos_t + x1 * sin_t)
    return jnp.concatenate(outs, axis=0)


def _mixer_in_kernel(x_ref, *refs):
    _mixer_in_body(x_ref[0], *refs)


def _mixer_in_fused_kernel(pos_ref, pos_next_ref, x1_ref, mod_prev_ref, ys_ref, *refs):
    (*body_refs, x_ref, carry_ref, land0, land1, y_ref, sems) = refs
    tm = x1_ref.shape[1]
    g = LANE_GROUPS
    step = pl.program_id(0) * pl.num_programs(1) + pl.program_id(1)
    last = pl.num_programs(0) * pl.num_programs(1) - 1
    landing = (land0, land1)

    def gather(idx_ref, slot, method):
        def row_copy(base, j):
            src = pl.multiple_of(idx_ref[0, 0, base + j] * g, g)
            return pltpu.make_async_copy(ys_ref.at[pl.ds(src, g)], landing[slot].at[pl.ds((base + j) * g, g)],
                                         sems.at[slot])
        _for_rows(tm, row_copy, method)

    pl.when(step == 0)(lambda: gather(pos_ref, 0, "start"))
    for slot in range(2):
        pl.when((step < last) & ((step + 1) % 2 == slot))(functools.partial(gather, pos_next_ref, slot, "start"))
    for slot in range(2):
        @pl.when(step % 2 == slot)
        def _(slot=slot):
            gather(pos_ref, slot, "wait")
            y_ref[...] = jnp.concatenate([landing[slot][pl.ds(c, tm, stride=g), :] for c in range(g)], axis=1)

    x = x1_ref[0] + mod_prev_ref[0][5:6] * y_ref[...]
    x_ref[0] = x
    _mixer_in_body(x, *body_refs, carry_ref)


def _mixer_in_body(x, mod_ref, g1_ref, win_ref, cw_ref, cg_ref, qg_ref, kg_ref, cos_ref, sin_ref,
                   yc_ref, qt_ref, k_ref, vt_ref, km_ref, carry_ref):
    tm = x.shape[0]

    @pl.when(pl.program_id(1) == 0)
    def _():
        carry_ref[...] = jnp.zeros_like(carry_ref)

    mod = mod_ref[0]
    h = _rmsnorm(x, g1_ref[...]) * (1.0 + mod[1:2]) + mod[0:1]
    u = jnp.dot(h.astype(BF16), win_ref[...], preferred_element_type=F32)

    cw = CONV_WIDTH
    u_b, gated = u[:, :cw], u[:, cw:2 * cw] * u[:, 2 * cw:3 * cw]
    row = lax.broadcasted_iota(jnp.int32, (tm, 1), 0)
    tail = carry_ref[...]
    prev1 = jnp.where(row == 0, tail[7:8], pltpu.roll(gated, 1, 0))
    prev2 = jnp.where(row == 0, tail[6:7], jnp.where(row == 1, tail[7:8], pltpu.roll(gated, 2, 0)))
    taps = cw_ref[...]
    y_conv = u_b * (taps[0:1] * prev2 + taps[1:2] * prev1 + taps[2:3] * gated)
    carry_ref[...] = gated[tm - 8:]
    yc_ref[0] = _rmsnorm(y_conv, cg_ref[...]).astype(BF16)

    a0 = 3 * cw
    cos_t, sin_t = cos_ref[0], sin_ref[0]
    q_t = _headnorm_rope_t(u[:, a0:a0 + ATT_WIDTH].T, qg_ref[...], cos_t, sin_t)
    k_t = _headnorm_rope_t(u[:, a0 + ATT_WIDTH:a0 + 2 * ATT_WIDTH].T, kg_ref[...], cos_t, sin_t)
    k = k_t.T
    qt_ref[0] = (q_t * Q_SCALE).astype(BF16)
    k_ref[0] = k.astype(BF16)
    km_ref[0, 0] = jnp.mean(k, axis=0, keepdims=True)
    vt_ref[0] = u[:, a0 + 2 * ATT_WIDTH:].T.astype(BF16)


def _mixer_in(x, layer_args, combine=None):
    bsz, seq, d = x.shape
    tm = TOKEN_TILE
    nb = seq // tm
    const2 = lambda b, t: (0, 0)
    tok = pl.BlockSpec((1, tm, d), lambda b, t: (b, t, 0))
    per_batch = pl.BlockSpec((1, 6, d), lambda b, t: (b, 0, 0))
    transposed = pl.BlockSpec((1, ATT_WIDTH, tm), lambda b, t: (b, 0, t))
    in_specs = [
        per_batch,
        pl.BlockSpec((1, d), const2),
        pl.BlockSpec((d, IN_WIDTH), const2),
        pl.BlockSpec((CONV_K, CONV_WIDTH), const2),
        pl.BlockSpec((1, CONV_WIDTH), const2),
        pl.BlockSpec((HEAD_DIM, tm), const2),
        pl.BlockSpec((HEAD_DIM, tm), const2),
        pl.BlockSpec((1, HALF_DIM, tm), lambda b, t: (b, 0, t)),
        pl.BlockSpec((1, HALF_DIM, tm), lambda b, t: (b, 0, t)),
    ]
    out_specs = [
        pl.BlockSpec((1, tm, CONV_WIDTH), lambda b, t: (b, t, 0)),
        transposed,
        pl.BlockSpec((1, tm, ATT_WIDTH), lambda b, t: (b, t, 0)),
        transposed,
        pl.BlockSpec((1, 1, 1, ATT_WIDTH), lambda b, t: (b, t, 0, 0)),
    ]
    out_shape = [
        jax.ShapeDtypeStruct((bsz, seq, CONV_WIDTH), BF16),
        jax.ShapeDtypeStruct((bsz, ATT_WIDTH, seq), BF16),
        jax.ShapeDtypeStruct((bsz, seq, ATT_WIDTH), BF16),
        jax.ShapeDtypeStruct((bsz, ATT_WIDTH, seq), BF16),
        jax.ShapeDtypeStruct((bsz, nb, 1, ATT_WIDTH), F32),
    ]
    scratch = [pltpu.VMEM((8, CONV_WIDTH), F32)]
    if combine is None:
        return pl.pallas_call(
            _mixer_in_kernel, grid=(bsz, nb), in_specs=[tok] + in_specs, out_specs=out_specs,
            out_shape=out_shape, scratch_shapes=scratch, compiler_params=_params(2), name="mixer_in",
        )(x, *layer_args)

    pos, mod_prev, y_sorted = combine
    n_blocks = bsz * nb
    pos_blocks = pos.reshape(n_blocks, 1, tm)
    landing = pltpu.VMEM((tm * LANE_GROUPS, LANES), F32)
    return pl.pallas_call(
        _mixer_in_fused_kernel,
        grid=(bsz, nb),
        in_specs=[
            pl.BlockSpec((1, 1, tm), lambda b, t: (b * nb + t, 0, 0), memory_space=pltpu.SMEM),
            pl.BlockSpec((1, 1, tm), lambda b, t: (jnp.minimum(b * nb + t + 1, n_blocks - 1), 0, 0),
                         memory_space=pltpu.SMEM),
            tok, per_batch, pl.BlockSpec(memory_space=pl.ANY),
        ] + in_specs,
        out_specs=out_specs + [tok],
        out_shape=out_shape + [jax.ShapeDtypeStruct((bsz, seq, d), F32)],
        scratch_shapes=scratch + [landing, landing, pltpu.VMEM((tm, d), F32), pltpu.SemaphoreType.DMA((2,))],
        compiler_params=_params(2),
        name="mixer_in_combine",
    )(pos_blocks, pos_blocks, x, mod_prev, y_sorted, *layer_args)


def _block_bias(q_t, km, own, n_sel):
    tq = q_t.shape[-1]
    if own <= n_sel:
        return None
    km = km[:own]
    km_hi = km.astype(BF16).astype(F32)
    km_lo = km - km_hi
    first = lax.broadcasted_iota(jnp.int32, (1, HEAD_PAIR), 1) < HEAD_DIM
    km_rows = jnp.concatenate([jnp.where(first, km_hi, 0.0), jnp.where(first, 0.0, km_hi),
                               jnp.where(first, km_lo, 0.0), jnp.where(first, 0.0, km_lo)], axis=0)
    gate = jnp.dot(km_rows.astype(BF16), q_t, preferred_element_type=F32)
    blk = lax.broadcasted_iota(jnp.int32, (own, tq), 0)
    biases = []
    for h in range(2):
        g = gate[h * own:(h + 1) * own] + gate[(2 + h) * own:(3 + h) * own]
        rank = jnp.zeros((own, tq), F32)
        for m in range(own):
            gm = g[m:m + 1]
            ahead = (gm > g) | ((gm == g) & (blk > m))
            rank = rank + jnp.where(ahead, 1.0, 0.0)
        biases.append(jnp.where(rank < n_sel, 0.0, NEG))
    return biases


def _attend(own, n_sel, tq, qt_ref, k_ref, vt_ref, km_ref, o_ref):
    q_t = qt_ref[0, :, own * tq:(own + 1) * tq]
    dead = jnp.zeros((HEAD_DIM, tq), BF16)
    q_wide = jnp.concatenate([jnp.concatenate([q_t[:HEAD_DIM], dead], axis=0),
                              jnp.concatenate([dead, q_t[HEAD_DIM:]], axis=0)], axis=1)
    biases = _block_bias(q_t, km_ref[0], own, n_sel)
    if biases is not None:
        biases = jnp.concatenate(biases, axis=1)
    causal = (lax.broadcasted_iota(jnp.int32, (tq, tq), 0) <= lax.broadcasted_iota(jnp.int32, (tq, tq), 1))
    causal = jnp.concatenate([causal, causal], axis=1)
    n_keys = (own + 1) * tq
    m = None
    scores = []
    for n in range(own + 1):
        s = jnp.dot(k_ref[0, n * tq:(n + 1) * tq, :], q_wide, preferred_element_type=F32)
        if n == own:
            s = jnp.where(causal, s, NEG)
        elif biases is not None:
            s = s + biases[n:n + 1]
        scores.append(s)
        blk_max = jnp.max(s, axis=0, keepdims=True)
        m = blk_max if m is None else jnp.maximum(m, blk_max)
    l = jnp.zeros((1, 2 * tq), F32)
    probs = []
    for s in scores:
        p = jnp.exp2(s - m)
        l = l + jnp.sum(p, axis=0, keepdims=True)
        probs.append(p.astype(BF16))
    acc = jnp.dot(vt_ref[0, :, :n_keys], jnp.concatenate(probs, axis=0), preferred_element_type=F32)
    acc = acc * (1.0 / l)
    o_t = jnp.concatenate([acc[:HEAD_DIM, :tq], acc[HEAD_DIM:, tq:]], axis=0)
    o_ref[0, own * tq:(own + 1) * tq, :] = o_t.T.astype(BF16)


def _attn_kernel(qt_ref, k_ref, vt_ref, km_ref, o_ref):
    tq = MOBA_BLOCK
    nb = k_ref.shape[1] // tq
    n_sel = max(1, min(MOBA_TOPK, nb - 1))
    for own in range(nb):
        _attend(own, n_sel, tq, qt_ref, k_ref, vt_ref, km_ref, o_ref)


def _attention(q_t, k, v_t, kmean):
    bsz, seq, _ = k.shape
    tq = MOBA_BLOCK
    nb = seq // tq
    return pl.pallas_call(
        _attn_kernel,
        grid=(bsz, N_PAIRS),
        in_specs=[
            pl.BlockSpec((1, HEAD_PAIR, seq), lambda b, p: (b, p, 0)),
            pl.BlockSpec((1, seq, HEAD_PAIR), lambda b, p: (b, 0, p)),
            pl.BlockSpec((1, HEAD_PAIR, seq), lambda b, p: (b, p, 0)),
            pl.BlockSpec((1, nb, HEAD_PAIR), lambda b, p: (b, 0, p)),
        ],
        out_specs=pl.BlockSpec((1, seq, HEAD_PAIR), lambda b, p: (b, 0, p)),
        out_shape=jax.ShapeDtypeStruct((bsz, seq, ATT_WIDTH), BF16),
        compiler_params=_params(2),
        name="moba_attention",
    )(q_t, k, v_t, kmean)


def _mixer_out_kernel(x_ref, yc_ref, ya_ref, mod_ref, ag_ref, wout_ref, g2_ref, wr_ref, br_ref, tri_ref,
                      x1_ref, hs_ref, meta_ref, cnt_ref):
    @pl.when((pl.program_id(0) == 0) & (pl.program_id(1) == 0))
    def _():
        cnt_ref[...] = jnp.zeros_like(cnt_ref)

    mod = mod_ref[0]
    ya = _rmsnorm(ya_ref[0].astype(F32), ag_ref[...]).astype(BF16)
    proj = (jnp.dot(yc_ref[0], wout_ref[:CONV_WIDTH], preferred_element_type=F32)
            + jnp.dot(ya, wout_ref[CONV_WIDTH:], preferred_element_type=F32))
    x1 = x_ref[0] + mod[2:3] * proj
    x1_ref[0] = x1
    h2 = _rmsnorm(x1, g2_ref[...]) * (1.0 + mod[4:5]) + mod[3:4]

    logits = jnp.dot(h2.astype(BF16), wr_ref[...], preferred_element_type=F32) + br_ref[...]
    lane = lax.broadcasted_iota(jnp.int32, (1, LANES), 1).astype(F32)
    is_group = lane < N_GROUPS
    gl = jnp.where(is_group, logits, LOWEST)
    g_max = jnp.max(gl, axis=-1, keepdims=True)
    g_sel = jnp.min(jnp.where(gl == g_max, lane, float(LANES)), axis=-1, keepdims=True)
    p_group = 1.0 / jnp.sum(jnp.where(is_group, jnp.exp(gl - g_max), 0.0), axis=-1, keepdims=True)

    e0 = N_GROUPS + EXPERTS_PER_GROUP * g_sel
    el = jnp.where((lane >= e0) & (lane < e0 + EXPERTS_PER_GROUP), logits, LOWEST)
    v1 = jnp.max(el, axis=-1, keepdims=True)
    i1 = jnp.min(jnp.where(el == v1, lane, float(LANES)), axis=-1, keepdims=True)
    el2 = jnp.where(lane == i1, LOWEST, el)
    v2 = jnp.max(el2, axis=-1, keepdims=True)
    i2 = jnp.min(jnp.where(el2 == v2, lane, float(LANES)), axis=-1, keepdims=True)
    ex = jnp.exp(v2 - v1)
    w_top1 = p_group / (1.0 + ex)
    w_top2 = w_top1 * ex

    a, b = i1 - e0, i2 - e0
    lo, hi = jnp.minimum(a, b), jnp.maximum(a, b)
    bucket = g_sel * PAIRS_PER_GROUP + lo * (7.0 - lo) * 0.5 + (hi - lo - 1.0)
    w_lo = jnp.where(a < b, w_top1, w_top2)
    w_hi = jnp.where(a < b, w_top2, w_top1)

    onehot = jnp.where(lane == bucket, 1.0, 0.0)
    before = jnp.dot(tri_ref[...], onehot.astype(BF16), preferred_element_type=F32)
    cnt = cnt_ref[...]
    rank = jnp.sum((before + cnt) * onehot, axis=-1, keepdims=True)
    cnt_ref[...] = cnt + jnp.sum(onehot, axis=0, keepdims=True)

    meta_ref[0] = jnp.where(lane == 0.0, bucket, jnp.where(lane == 1.0, rank, 0.0))
    hs_ref[0, :, :D_MODEL] = h2
    hs_ref[0, :, D_MODEL:] = jnp.where(lane == 0.0, w_lo, jnp.where(lane == 1.0, w_hi, 0.0))


def _mixer_out(x, yc, ya, mod_l, attn_g, w_out, g2, w_router, b_router, tri):
    bsz, seq, d = x.shape
    tm = TOKEN_TILE
    const2 = lambda b, t: (0, 0)
    tok = lambda width: pl.BlockSpec((1, tm, width), lambda b, t: (b, t, 0))
    return pl.pallas_call(
        _mixer_out_kernel,
        grid=(bsz, seq // tm),
        in_specs=[
            tok(d), tok(CONV_WIDTH), tok(ATT_WIDTH),
            pl.BlockSpec((1, 6, d), lambda b, t: (b, 0, 0)),
            pl.BlockSpec((1, ATT_WIDTH), const2),
            pl.BlockSpec((d, d), const2),
            pl.BlockSpec((1, d), const2),
            pl.BlockSpec((d, LANES), const2),
            pl.BlockSpec((1, LANES), const2),
            pl.BlockSpec((tm, tm), const2),
        ],
        out_specs=[tok(d), tok(ROW_WORDS), tok(LANES), pl.BlockSpec((1, LANES), const2)],
        out_shape=[
            jax.ShapeDtypeStruct((bsz, seq, d), F32),
            jax.ShapeDtypeStruct((bsz, seq, ROW_WORDS), F32),
            jax.ShapeDtypeStruct((bsz, seq, LANES), F32),
            jax.ShapeDtypeStruct((1, LANES), F32),
        ],
        compiler_params=_params(2),
        name="mixer_out",
    )(x, yc, ya, mod_l, attn_g, w_out, g2, w_router, b_router, tri)


def _move_rows(rows, row_copy):
    _for_rows(rows, row_copy, "start")
    _for_rows(rows, row_copy, "wait")


def _dispatch_kernel(pos_ref, src_ref, dst_in_ref, dst_ref, sem):
    del dst_in_ref

    def row_copy(base, j):
        return pltpu.make_async_copy(src_ref.at[pl.ds(base + j, 1)],
                                     dst_ref.at[pl.ds(pos_ref[0, 0, base + j], 1)], sem)

    _move_rows(pos_ref.shape[-1], row_copy)


def _dispatch(pos, rows_tok, sorted_init):
    n, width = rows_tok.shape
    n_sorted = sorted_init.shape[0]
    mt = MOVE_TILE
    return pl.pallas_call(
        _dispatch_kernel,
        grid=(n // mt,),
        in_specs=[
            pl.BlockSpec((1, 1, mt), lambda i: (i, 0, 0), memory_space=pltpu.SMEM),
            pl.BlockSpec((mt, width), lambda i: (i, 0)),
            pl.BlockSpec(memory_space=pl.ANY),
        ],
        out_specs=pl.BlockSpec(memory_space=pl.ANY),
        out_shape=jax.ShapeDtypeStruct((n_sorted, width), F32),
        scratch_shapes=[pltpu.SemaphoreType.DMA(())],
        input_output_aliases={2: 0},
        compiler_params=_params(1),
        name="moe_dispatch",
    )(pos.reshape(n // mt, 1, mt), rows_tok, sorted_init)


def _expert_kernel(lo_ref, hi_ref, live_ref, hs_ref, wgu_lo, wgu_hi, wd_lo, wd_hi, y_ref):
    del lo_ref, hi_ref
    t = pl.program_id(0)

    @pl.when(live_ref[t] != 0)
    def _():
        hs = hs_ref[...]
        h = hs[:, :D_MODEL].astype(BF16)

        def expert(wgu_ref, wd_ref, w):
            gu = jnp.dot(h, wgu_ref[0], preferred_element_type=F32)
            hid = _silu(gu[:, :D_FF_EXPERT]) * gu[:, D_FF_EXPERT:] * w
            return jnp.dot(hid.astype(BF16), wd_ref[0], preferred_element_type=F32)

        y = (expert(wgu_lo, wd_lo, hs[:, D_MODEL:D_MODEL + 1])
             + expert(wgu_hi, wd_hi, hs[:, D_MODEL + 1:D_MODEL + 2]))
        for c in range(LANE_GROUPS):
            y_ref[pl.ds(c, y.shape[0], stride=LANE_GROUPS), :] = y[:, c * LANES:(c + 1) * LANES]

    @pl.when(live_ref[t] == 0)
    def _():
        y_ref[...] = jnp.zeros_like(y_ref)


def _experts(tile_lo, tile_hi, tile_live, rows_sorted, w_gu, w_down):
    n_sorted = rows_sorted.shape[0]
    te = EXPERT_TILE
    grid_spec = pltpu.PrefetchScalarGridSpec(
        num_scalar_prefetch=3,
        grid=(n_sorted // te,),
        in_specs=[
            pl.BlockSpec((te, ROW_WORDS), lambda t, lo, hi, live: (t, 0)),
            pl.BlockSpec((1, D_MODEL, 2 * D_FF_EXPERT), lambda t, lo, hi, live: (lo[t], 0, 0)),
            pl.BlockSpec((1, D_MODEL, 2 * D_FF_EXPERT), lambda t, lo, hi, live: (hi[t], 0, 0)),
            pl.BlockSpec((1, D_FF_EXPERT, D_MODEL), lambda t, lo, hi, live: (lo[t], 0, 0)),
            pl.BlockSpec((1, D_FF_EXPERT, D_MODEL), lambda t, lo, hi, live: (hi[t], 0, 0)),
        ],
        out_specs=pl.BlockSpec((te * LANE_GROUPS, LANES), lambda t, lo, hi, live: (t, 0)),
    )
    return pl.pallas_call(
        _expert_kernel,
        grid_spec=grid_spec,
        out_shape=jax.ShapeDtypeStruct((n_sorted * LANE_GROUPS, LANES), F32),
        compiler_params=_params(1),
        name="moe_experts",
    )(tile_lo, tile_hi, tile_live, rows_sorted, w_gu, w_gu, w_down, w_down)


def _combine_kernel(pos_ref, x1_ref, mod_ref, ys_ref, o_ref, buf, sem):
    rows = pos_ref.shape[-1]
    g = LANE_GROUPS

    def row_copy(base, j):
        src = pl.multiple_of(pos_ref[0, 0, base + j] * g, g)
        return pltpu.make_async_copy(ys_ref.at[pl.ds(src, g)], buf.at[pl.ds((base + j) * g, g)], sem)

    _move_rows(rows, row_copy)
    y = jnp.concatenate([buf[pl.ds(c, rows, stride=g), :] for c in range(g)], axis=1)
    o_ref[0] = x1_ref[0] + mod_ref[0][5:6] * y


def _combine(pos, x1, mod_l, y_sorted):
    bsz, seq, d = x1.shape
    mt = MOVE_TILE
    per_seq = seq // mt
    return pl.pallas_call(
        _combine_kernel,
        grid=(bsz, per_seq),
        in_specs=[
            pl.BlockSpec((1, 1, mt), lambda b, t: (b * per_seq + t, 0, 0), memory_space=pltpu.SMEM),
            pl.BlockSpec((1, mt, d), lambda b, t: (b, t, 0)),
            pl.BlockSpec((1, 6, d), lambda b, t: (b, 0, 0)),
            pl.BlockSpec(memory_space=pl.ANY),
        ],
        out_specs=pl.BlockSpec((1, mt, d), lambda b, t: (b, t, 0)),
        out_shape=jax.ShapeDtypeStruct((bsz, seq, d), F32),
        scratch_shapes=[pltpu.VMEM((mt * LANE_GROUPS, LANES), F32), pltpu.SemaphoreType.DMA(())],
        compiler_params=_params(2),
        name="moe_combine",
    )(pos.reshape(bsz * per_seq, 1, mt), x1, mod_l, y_sorted)


def _routing_tables(meta, counts, n_tiles):
    te = EXPERT_TILE
    bucket = meta[:, 0].astype(jnp.int32)
    rank = meta[:, 1].astype(jnp.int32)
    counts = counts[0, :N_BUCKETS].astype(jnp.int32)
    tiles_per = (counts + te - 1) // te
    tile_end = jnp.cumsum(tiles_per)
    offsets = (tile_end - tiles_per) * te
    pos = offsets[bucket] + rank
    tile_ids = jnp.arange(n_tiles, dtype=jnp.int32)
    live = tile_ids < tile_end[-1]
    last_live = jnp.maximum(tile_end[-1] - 1, 0)
    tile_bucket = jnp.sum(jnp.minimum(tile_ids, last_live)[:, None] >= tile_end[None, :], axis=1, dtype=jnp.int32)
    tile_bucket = jnp.minimum(tile_bucket, N_BUCKETS - 1)
    pair_lo = jnp.array([0, 0, 0, 1, 1, 2], jnp.int32)
    pair_hi = jnp.array([1, 2, 3, 2, 3, 3], jnp.int32)
    group0 = (tile_bucket // PAIRS_PER_GROUP) * EXPERTS_PER_GROUP
    pair = tile_bucket % PAIRS_PER_GROUP
    return pos, group0 + pair_lo[pair], group0 + pair_hi[pair], live.astype(jnp.int32)


def kernel(x, c, positions, norm1_g, norm2_g, w_ada, b_ada, w_in, conv_w, q_norm_g, k_norm_g, conv_out_g,
           attn_out_g, w_out, w_router_group, b_router_group, w_router_expert, b_router_expert,
           w_gate, w_up, w_down):
    bsz, seq, d = x.shape
    depth = w_in.shape[0]
    n = bsz * seq
    tm = TOKEN_TILE
    n_tiles = n // EXPERT_TILE + N_BUCKETS
    n_sorted = n_tiles * EXPERT_TILE

    mod = _modulation(c, w_ada, b_ada).reshape(depth, bsz, 6, d)
    cos_t, sin_t = _rope_tables(positions)

    w_in_b = w_in.astype(BF16)
    w_out_b = w_out.astype(BF16)
    w_gu_b = jnp.concatenate([w_gate, w_up], axis=-1).astype(BF16)
    w_down_b = w_down.astype(BF16)
    pad = LANES - N_GROUPS - N_EXPERTS
    w_router = jnp.concatenate(
        [w_router_group, w_router_expert, jnp.zeros((depth, d, pad), F32)], axis=-1).astype(BF16)
    b_router = jnp.concatenate(
        [b_router_group, b_router_expert, jnp.zeros((depth, pad), F32)], axis=-1).reshape(depth, 1, LANES)
    q_gain = jnp.broadcast_to(q_norm_g[:, :, None], (depth, HEAD_DIM, tm))
    k_gain = jnp.broadcast_to(k_norm_g[:, :, None], (depth, HEAD_DIM, tm))
    idx = jnp.arange(tm)
    tri = (idx[None, :] < idx[:, None]).astype(BF16)

    rows_sorted = jnp.zeros((n_sorted, ROW_WORDS), F32)
    pending = None
    for l in range(depth):
        layer_args = (mod[l], norm1_g[l][None], w_in_b[l], conv_w[l], conv_out_g[l][None],
                      q_gain[l], k_gain[l], cos_t, sin_t)
        if pending is None:
            yc, q_t, k, v_t, kmean = _mixer_in(x, layer_args)
        else:
            yc, q_t, k, v_t, kmean, x = _mixer_in(x1, layer_args, combine=pending)
        ya = _attention(q_t, k, v_t, kmean.reshape(bsz, seq // tm, ATT_WIDTH))
        x1, rows_tok, meta, counts = _mixer_out(
            x, yc, ya, mod[l], attn_out_g[l][None], w_out_b[l], norm2_g[l][None],
            w_router[l], b_router[l], tri)
        pos, tile_lo, tile_hi, tile_live = _routing_tables(meta.reshape(n, LANES), counts, n_tiles)
        rows_sorted = _dispatch(pos, rows_tok.reshape(n, ROW_WORDS), rows_sorted)
        y_sorted = _experts(tile_lo, tile_hi, tile_live, rows_sorted, w_gu_b[l], w_down_b[l])
        pending = (pos, mod[l], y_sorted)
    return _combine(pending[0], x1, pending[1], pending[2])
```

```python
import functools

import jax
import jax.numpy as jnp
from jax import lax
from jax.experimental import pallas as pl
from jax.experimental.pallas import tpu as pltpu

F32 = jnp.float32
BF16 = jnp.bfloat16

D_MODEL = 1024
CONV_WIDTH = 512
CONV_K = 3
HEAD_DIM = 64
HALF_DIM = HEAD_DIM // 2
ATT_WIDTH = 512
ATT_HEADS = ATT_WIDTH // HEAD_DIM
IN_WIDTH = 3 * CONV_WIDTH + 3 * ATT_WIDTH
MOBA_BLOCK = 256
MOBA_TOPK = 3
ROPE_THETA = 10000.0
N_GROUPS = 4
EXPERTS_PER_GROUP = 4
N_EXPERTS = N_GROUPS * EXPERTS_PER_GROUP
D_FF_EXPERT = D_MODEL // 4
EPS = 1e-6
NEG = -1e30
LOWEST = -3e38
LOG2_E = 1.4426950408889634
Q_SCALE = HEAD_DIM ** -0.5 * LOG2_E

LANES = 128
LANE_GROUPS = D_MODEL // LANES
HEAD_PAIR = 2 * HEAD_DIM
N_PAIRS = ATT_WIDTH // HEAD_PAIR
VT_ROWS = HEAD_PAIR + 16
TOKEN_TILE = MOBA_BLOCK
PAIRS_PER_GROUP = 6
N_BUCKETS = N_GROUPS * PAIRS_PER_GROUP
EXPERT_TILE = 256
ROW_WORDS = D_MODEL + LANES
MOVE_TILE = 512
ROW_UNROLL = 8
VMEM_LIMIT = 48 * 1024 * 1024


def _params(n_axes, vmem=VMEM_LIMIT):
    return pltpu.CompilerParams(dimension_semantics=("arbitrary",) * n_axes, vmem_limit_bytes=vmem)


def _rmsnorm(x, g):
    return x * lax.rsqrt(jnp.mean(x * x, axis=-1, keepdims=True) + EPS) * g


def _silu(x):
    return x * (1.0 / (1.0 + jnp.exp(-x)))


def _for_rows(rows, row_copy, method):
    def body(g, carry):
        base = pl.multiple_of(g * ROW_UNROLL, ROW_UNROLL)
        for j in range(ROW_UNROLL):
            getattr(row_copy(base, j), method)()
        return carry

    lax.fori_loop(0, rows // ROW_UNROLL, body, 0)


def _mod_kernel(c_ref, w_ref, b_ref, o_ref):
    ca = _silu(c_ref[...]).astype(BF16)
    o_ref[0] = jnp.dot(ca, w_ref[0].astype(BF16), preferred_element_type=F32) + b_ref[0]


def _modulation(c, w_ada, b_ada):
    depth, d, width = w_ada.shape
    bsz = c.shape[0]
    tn = 1536
    return pl.pallas_call(
        _mod_kernel,
        grid=(depth, width // tn),
        in_specs=[
            pl.BlockSpec((bsz, d), lambda l, j: (0, 0)),
            pl.BlockSpec((1, d, tn), lambda l, j: (l, 0, j)),
            pl.BlockSpec((1, 1, tn), lambda l, j: (l, 0, j)),
        ],
        out_specs=pl.BlockSpec((1, bsz, tn), lambda l, j: (l, 0, j)),
        out_shape=jax.ShapeDtypeStruct((depth, bsz, width), F32),
        compiler_params=_params(2),
        name="modulation",
    )(c, w_ada, b_ada.reshape(depth, 1, width))


def _rope_kernel(pos_ref, freq_ref, cos_ref, sin_ref):
    ts = pos_ref.shape[-1]
    freq = jnp.concatenate([freq_ref[...]] * (ts // LANES), axis=1)
    ang = pos_ref[0] * freq
    cos_ref[0] = jnp.cos(ang)
    sin_ref[0] = jnp.sin(ang)


def _rope_tables(positions):
    bsz, seq = positions.shape
    ts = 512
    inv_freq = ROPE_THETA ** (-jnp.arange(0, HEAD_DIM, 2, dtype=F32) / HEAD_DIM)
    freq = jnp.broadcast_to(inv_freq[:, None], (HALF_DIM, LANES))
    pos = positions.astype(F32).reshape(bsz, 1, seq)
    spec = pl.BlockSpec((1, HALF_DIM, ts), lambda b, t: (b, 0, t))
    return pl.pallas_call(
        _rope_kernel,
        grid=(bsz, seq // ts),
        in_specs=[pl.BlockSpec((1, 1, ts), lambda b, t: (b, 0, t)),
                  pl.BlockSpec((HALF_DIM, LANES), lambda b, t: (0, 0))],
        out_specs=[spec, spec],
        out_shape=[jax.ShapeDtypeStruct((bsz, HALF_DIM, seq), F32)] * 2,
        compiler_params=_params(2),
        name="rope_tables",
    )(pos, freq)


def _headnorm_rope_t(z_t, gain, cos_t, sin_t):
    outs = []
    for hd in range(ATT_HEADS):
        z = z_t[hd * HEAD_DIM:(hd + 1) * HEAD_DIM]
        zn = z * lax.rsqrt(jnp.mean(z * z, axis=0, keepdims=True) + EPS) * gain
        x1, x2 = zn[:HALF_DIM], zn[HALF_DIM:]
        outs.append(x1 * cos_t - x2 * sin_t)
        outs.append(x2 * cos_t + x1 * sin_t)
    return jnp.concatenate(outs, axis=0)


def _mixer_in_kernel(x_ref, *refs):
    _mixer_in_body(x_ref[0], *refs)


def _mixer_in_fused_kernel(pos_ref, pos_next_ref, x1_ref, mod_prev_ref, ys_ref, *refs):
    (*body_refs, x_ref, carry_ref, land0, land1, y_ref, sems) = refs
    tm = x1_ref.shape[1]
    g = LANE_GROUPS
    step = pl.program_id(0) * pl.num_programs(1) + pl.program_id(1)
    last = pl.num_programs(0) * pl.num_programs(1) - 1
    landing = (land0, land1)

    def gather(idx_ref, slot, method):
        def row_copy(base, j):
            src = pl.multiple_of(idx_ref[0, 0, base + j] * g, g)
            return pltpu.make_async_copy(ys_ref.at[pl.ds(src, g)], landing[slot].at[pl.ds((base + j) * g, g)],
                                         sems.at[slot])
        _for_rows(tm, row_copy, method)

    pl.when(step == 0)(lambda: gather(pos_ref, 0, "start"))
    for slot in range(2):
        pl.when((step < last) & ((step + 1) % 2 == slot))(functools.partial(gather, pos_next_ref, slot, "start"))
    for slot in range(2):
        @pl.when(step % 2 == slot)
        def _(slot=slot):
            gather(pos_ref, slot, "wait")
            y_ref[...] = jnp.concatenate([landing[slot][pl.ds(c, tm, stride=g), :] for c in range(g)], axis=1)

    x = x1_ref[0] + mod_prev_ref[0][5:6] * y_ref[...]
    x_ref[0] = x
    _mixer_in_body(x, *body_refs, carry_ref)


def _mixer_in_body(x, mod_ref, g1_ref, win_ref, cw_ref, cg_ref, qg_ref, kg_ref, cos_ref, sin_ref,
                   yc_ref, qt_ref, k_ref, vt_ref, km_ref, carry_ref):
    tm = x.shape[0]

    @pl.when(pl.program_id(1) == 0)
    def _():
        carry_ref[...] = jnp.zeros_like(carry_ref)

    mod = mod_ref[0]
    h = _rmsnorm(x, g1_ref[...]) * (1.0 + mod[1:2]) + mod[0:1]
    u = jnp.dot(h.astype(BF16), win_ref[...], preferred_element_type=F32)

    cw = CONV_WIDTH
    u_b, gated = u[:, :cw], u[:, cw:2 * cw] * u[:, 2 * cw:3 * cw]
    row = lax.broadcasted_iota(jnp.int32, (tm, 1), 0)
    tail = carry_ref[...]
    prev1 = jnp.where(row == 0, tail[7:8], pltpu.roll(gated, 1, 0))
    prev2 = jnp.where(row == 0, tail[6:7], jnp.where(row == 1, tail[7:8], pltpu.roll(gated, 2, 0)))
    taps = cw_ref[...]
    y_conv = u_b * (taps[0:1] * prev2 + taps[1:2] * prev1 + taps[2:3] * gated)
    carry_ref[...] = gated[tm - 8:]
    yc_ref[0] = _rmsnorm(y_conv, cg_ref[...]).astype(BF16)

    a0 = 3 * cw
    cos_t, sin_t = cos_ref[0], sin_ref[0]
    q_t = _headnorm_rope_t(u[:, a0:a0 + ATT_WIDTH].T, qg_ref[...], cos_t, sin_t)
    k_t = _headnorm_rope_t(u[:, a0 + ATT_WIDTH:a0 + 2 * ATT_WIDTH].T, kg_ref[...], cos_t, sin_t)
    k = k_t.T
    qt_ref[0] = (q_t * Q_SCALE).astype(BF16)
    k_ref[0] = k.astype(BF16)
    km_ref[0, 0] = jnp.mean(k, axis=0, keepdims=True)
    v_t = u[:, a0 + 2 * ATT_WIDTH:].T
    ones = jnp.ones((VT_ROWS - HEAD_PAIR, tm), F32)
    vt_ref[0] = jnp.concatenate(
        [piece for p in range(N_PAIRS) for piece in (v_t[p * HEAD_PAIR:(p + 1) * HEAD_PAIR], ones)],
        axis=0).astype(BF16)


def _mixer_in(x, layer_args, combine=None):
    bsz, seq, d = x.shape
    tm = TOKEN_TILE
    nb = seq // tm
    const2 = lambda b, t: (0, 0)
    tok = pl.BlockSpec((1, tm, d), lambda b, t: (b, t, 0))
    per_batch = pl.BlockSpec((1, 6, d), lambda b, t: (b, 0, 0))
    transposed = pl.BlockSpec((1, ATT_WIDTH, tm), lambda b, t: (b, 0, t))
    in_specs = [
        per_batch,
        pl.BlockSpec((1, d), const2),
        pl.BlockSpec((d, IN_WIDTH), const2),
        pl.BlockSpec((CONV_K, CONV_WIDTH), const2),
        pl.BlockSpec((1, CONV_WIDTH), const2),
        pl.BlockSpec((HEAD_DIM, tm), const2),
        pl.BlockSpec((HEAD_DIM, tm), const2),
        pl.BlockSpec((1, HALF_DIM, tm), lambda b, t: (b, 0, t)),
        pl.BlockSpec((1, HALF_DIM, tm), lambda b, t: (b, 0, t)),
    ]
    out_specs = [
        pl.BlockSpec((1, tm, CONV_WIDTH), lambda b, t: (b, t, 0)),
        transposed,
        pl.BlockSpec((1, tm, ATT_WIDTH), lambda b, t: (b, t, 0)),
        pl.BlockSpec((1, N_PAIRS * VT_ROWS, tm), lambda b, t: (b, 0, t)),
        pl.BlockSpec((1, 1, 1, ATT_WIDTH), lambda b, t: (b, t, 0, 0)),
    ]
    out_shape = [
        jax.ShapeDtypeStruct((bsz, seq, CONV_WIDTH), BF16),
        jax.ShapeDtypeStruct((bsz, ATT_WIDTH, seq), BF16),
        jax.ShapeDtypeStruct((bsz, seq, ATT_WIDTH), BF16),
        jax.ShapeDtypeStruct((bsz, N_PAIRS * VT_ROWS, seq), BF16),
        jax.ShapeDtypeStruct((bsz, nb, 1, ATT_WIDTH), F32),
    ]
    scratch = [pltpu.VMEM((8, CONV_WIDTH), F32)]
    if combine is None:
        return pl.pallas_call(
            _mixer_in_kernel, grid=(bsz, nb), in_specs=[tok] + in_specs, out_specs=out_specs,
            out_shape=out_shape, scratch_shapes=scratch, compiler_params=_params(2), name="mixer_in",
        )(x, *layer_args)

    pos, mod_prev, y_sorted = combine
    n_blocks = bsz * nb
    pos_blocks = pos.reshape(n_blocks, 1, tm)
    landing = pltpu.VMEM((tm * LANE_GROUPS, LANES), F32)
    return pl.pallas_call(
        _mixer_in_fused_kernel,
        grid=(bsz, nb),
        in_specs=[
            pl.BlockSpec((1, 1, tm), lambda b, t: (b * nb + t, 0, 0), memory_space=pltpu.SMEM),
            pl.BlockSpec((1, 1, tm), lambda b, t: (jnp.minimum(b * nb + t + 1, n_blocks - 1), 0, 0),
                         memory_space=pltpu.SMEM),
            tok, per_batch, pl.BlockSpec(memory_space=pl.ANY),
        ] + in_specs,
        out_specs=out_specs + [tok],
        out_shape=out_shape + [jax.ShapeDtypeStruct((bsz, seq, d), F32)],
        scratch_shapes=scratch + [landing, landing, pltpu.VMEM((tm, d), F32), pltpu.SemaphoreType.DMA((2,))],
        compiler_params=_params(2),
        name="mixer_in_combine",
    )(pos_blocks, pos_blocks, x, mod_prev, y_sorted, *layer_args)


def _block_bias(q_t, km, own, n_sel):
    tq = q_t.shape[-1]
    if own <= n_sel:
        return None
    km = km[:own]
    km_hi = km.astype(BF16).astype(F32)
    km_lo = km - km_hi
    first = lax.broadcasted_iota(jnp.int32, (1, HEAD_PAIR), 1) < HEAD_DIM
    km_rows = jnp.concatenate([jnp.where(first, km_hi, 0.0), jnp.where(first, 0.0, km_hi),
                               jnp.where(first, km_lo, 0.0), jnp.where(first, 0.0, km_lo)], axis=0)
    gate = jnp.dot(km_rows.astype(BF16), q_t, preferred_element_type=F32)
    blk = lax.broadcasted_iota(jnp.int32, (own, tq), 0)
    biases = []
    for h in range(2):
        g = gate[h * own:(h + 1) * own] + gate[(2 + h) * own:(3 + h) * own]
        rank = jnp.zeros((own, tq), F32)
        for m in range(own):
            gm = g[m:m + 1]
            ahead = (gm > g) | ((gm == g) & (blk > m))
            rank = rank + jnp.where(ahead, 1.0, 0.0)
        biases.append(jnp.where(rank < n_sel, 0.0, NEG))
    return biases


def _attend_scores(own, n_sel, tq, qt_ref, k_ref, km_ref):
    q_t = qt_ref[0, :, own * tq:(own + 1) * tq]
    dead = jnp.zeros((HEAD_DIM, tq), BF16)
    q_wide = jnp.concatenate([jnp.concatenate([q_t[:HEAD_DIM], dead], axis=0),
                              jnp.concatenate([dead, q_t[HEAD_DIM:]], axis=0)], axis=1)
    biases = _block_bias(q_t, km_ref[0], own, n_sel)
    if biases is not None:
        biases = jnp.concatenate(biases, axis=1)
    causal = (lax.broadcasted_iota(jnp.int32, (tq, tq), 0) <= lax.broadcasted_iota(jnp.int32, (tq, tq), 1))
    causal = jnp.concatenate([causal, causal], axis=1)
    m = None
    scores = []
    for n in range(own + 1):
        s = jnp.dot(k_ref[0, n * tq:(n + 1) * tq, :], q_wide, preferred_element_type=F32)
        bias = None
        if n == own:
            s = jnp.where(causal, s, NEG)
        elif biases is not None:
            bias = biases[n:n + 1]
        scores.append((s, bias))
        blk_max = jnp.max(s, axis=0, keepdims=True)
        if bias is not None:
            blk_max = blk_max + bias
        m = blk_max if m is None else jnp.maximum(m, blk_max)
    return scores, m


def _attend_finish(own, tq, scores, m, vt_ref, o_ref):
    n_keys = (own + 1) * tq
    probs = []
    for s, bias in scores:
        shift = m if bias is None else m - bias
        probs.append(jnp.exp2(s - shift).astype(BF16))
    acc = jnp.dot(vt_ref[0, :, :n_keys], jnp.concatenate(probs, axis=0), preferred_element_type=F32)
    acc = acc[:HEAD_PAIR] * (1.0 / acc[HEAD_PAIR:HEAD_PAIR + 1])
    o_t = jnp.concatenate([acc[:HEAD_DIM, :tq], acc[HEAD_DIM:, tq:]], axis=0)
    o_ref[0, own * tq:(own + 1) * tq, :] = o_t.T.astype(BF16)


def _attn_kernel(qt_ref, k_ref, vt_ref, km_ref, o_ref):
    tq = MOBA_BLOCK
    nb = k_ref.shape[1] // tq
    n_sel = max(1, min(MOBA_TOPK, nb - 1))
    ahead = _attend_scores(0, n_sel, tq, qt_ref, k_ref, km_ref)
    for own in range(nb):
        scores, m = ahead
        if own + 1 < nb:
            ahead = _attend_scores(own + 1, n_sel, tq, qt_ref, k_ref, km_ref)
        _attend_finish(own, tq, scores, m, vt_ref, o_ref)


def _attention(q_t, k, v_t, kmean):
    bsz, seq, _ = k.shape
    tq = MOBA_BLOCK
    nb = seq // tq
    return pl.pallas_call(
        _attn_kernel,
        grid=(bsz, N_PAIRS),
        in_specs=[
            pl.BlockSpec((1, HEAD_PAIR, seq), lambda b, p: (b, p, 0)),
            pl.BlockSpec((1, seq, HEAD_PAIR), lambda b, p: (b, 0, p)),
            pl.BlockSpec((1, VT_ROWS, seq), lambda b, p: (b, p, 0)),
            pl.BlockSpec((1, nb, HEAD_PAIR), lambda b, p: (b, 0, p)),
        ],
        out_specs=pl.BlockSpec((1, seq, HEAD_PAIR), lambda b, p: (b, 0, p)),
        out_shape=jax.ShapeDtypeStruct((bsz, seq, ATT_WIDTH), BF16),
        compiler_params=_params(2),
        name="moba_attention",
    )(q_t, k, v_t, kmean)


def _mixer_out_kernel(x_ref, yc_ref, ya_ref, mod_ref, ag_ref, wout_ref, g2_ref, wr_ref, br_ref, tri_ref,
                      x1_ref, hs_ref, meta_ref, cnt_ref):
    @pl.when((pl.program_id(0) == 0) & (pl.program_id(1) == 0))
    def _():
        cnt_ref[...] = jnp.zeros_like(cnt_ref)

    mod = mod_ref[0]
    ya = _rmsnorm(ya_ref[0].astype(F32), ag_ref[...]).astype(BF16)
    proj = (jnp.dot(yc_ref[0], wout_ref[:CONV_WIDTH], preferred_element_type=F32)
            + jnp.dot(ya, wout_ref[CONV_WIDTH:], preferred_element_type=F32))
    x1 = x_ref[0] + mod[2:3] * proj
    x1_ref[0] = x1
    h2 = _rmsnorm(x1, g2_ref[...]) * (1.0 + mod[4:5]) + mod[3:4]

    logits = jnp.dot(h2.astype(BF16), wr_ref[...], preferred_element_type=F32) + br_ref[...]
    lane = lax.broadcasted_iota(jnp.int32, (1, LANES), 1).astype(F32)
    is_group = lane < N_GROUPS
    gl = jnp.where(is_group, logits, LOWEST)
    g_max = jnp.max(gl, axis=-1, keepdims=True)
    g_sel = jnp.min(jnp.where(gl == g_max, lane, float(LANES)), axis=-1, keepdims=True)
    p_group = 1.0 / jnp.sum(jnp.where(is_group, jnp.exp(gl - g_max), 0.0), axis=-1, keepdims=True)

    e0 = N_GROUPS + EXPERTS_PER_GROUP * g_sel
    el = jnp.where((lane >= e0) & (lane < e0 + EXPERTS_PER_GROUP), logits, LOWEST)
    v1 = jnp.max(el, axis=-1, keepdims=True)
    i1 = jnp.min(jnp.where(el == v1, lane, float(LANES)), axis=-1, keepdims=True)
    el2 = jnp.where(lane == i1, LOWEST, el)
    v2 = jnp.max(el2, axis=-1, keepdims=True)
    i2 = jnp.min(jnp.where(el2 == v2, lane, float(LANES)), axis=-1, keepdims=True)
    ex = jnp.exp(v2 - v1)
    w_top1 = p_group / (1.0 + ex)
    w_top2 = w_top1 * ex

    a, b = i1 - e0, i2 - e0
    lo, hi = jnp.minimum(a, b), jnp.maximum(a, b)
    bucket = g_sel * PAIRS_PER_GROUP + lo * (7.0 - lo) * 0.5 + (hi - lo - 1.0)
    w_lo = jnp.where(a < b, w_top1, w_top2)
    w_hi = jnp.where(a < b, w_top2, w_top1)

    onehot = jnp.where(lane == bucket, 1.0, 0.0)
    before = jnp.dot(tri_ref[...], onehot.astype(BF16), preferred_element_type=F32)
    cnt = cnt_ref[...]
    rank = jnp.sum((before + cnt) * onehot, axis=-1, keepdims=True)
    cnt_ref[...] = cnt + jnp.sum(onehot, axis=0, keepdims=True)

    meta_ref[0] = jnp.where(lane == 0.0, bucket, jnp.where(lane == 1.0, rank, 0.0))
    hs_ref[0, :, :D_MODEL] = h2
    hs_ref[0, :, D_MODEL:] = jnp.where(lane == 0.0, w_lo, jnp.where(lane == 1.0, w_hi, 0.0))


def _mixer_out(x, yc, ya, mod_l, attn_g, w_out, g2, w_router, b_router, tri):
    bsz, seq, d = x.shape
    tm = TOKEN_TILE
    const2 = lambda b, t: (0, 0)
    tok = lambda width: pl.BlockSpec((1, tm, width), lambda b, t: (b, t, 0))
    return pl.pallas_call(
        _mixer_out_kernel,
        grid=(bsz, seq // tm),
        in_specs=[
            tok(d), tok(CONV_WIDTH), tok(ATT_WIDTH),
            pl.BlockSpec((1, 6, d), lambda b, t: (b, 0, 0)),
            pl.BlockSpec((1, ATT_WIDTH), const2),
            pl.BlockSpec((d, d), const2),
            pl.BlockSpec((1, d), const2),
            pl.BlockSpec((d, LANES), const2),
            pl.BlockSpec((1, LANES), const2),
            pl.BlockSpec((tm, tm), const2),
        ],
        out_specs=[tok(d), tok(ROW_WORDS), tok(LANES), pl.BlockSpec((1, LANES), const2)],
        out_shape=[
            jax.ShapeDtypeStruct((bsz, seq, d), F32),
            jax.ShapeDtypeStruct((bsz, seq, ROW_WORDS), F32),
            jax.ShapeDtypeStruct((bsz, seq, LANES), F32),
            jax.ShapeDtypeStruct((1, LANES), F32),
        ],
        compiler_params=_params(2),
        name="mixer_out",
    )(x, yc, ya, mod_l, attn_g, w_out, g2, w_router, b_router, tri)


def _move_rows(rows, row_copy):
    _for_rows(rows, row_copy, "start")
    _for_rows(rows, row_copy, "wait")


def _dispatch_kernel(pos_ref, src_ref, dst_in_ref, dst_ref, sem):
    del dst_in_ref

    def row_copy(base, j):
        return pltpu.make_async_copy(src_ref.at[pl.ds(base + j, 1)],
                                     dst_ref.at[pl.ds(pos_ref[0, 0, base + j], 1)], sem)

    _move_rows(pos_ref.shape[-1], row_copy)


def _dispatch(pos, rows_tok, sorted_init):
    n, width = rows_tok.shape
    n_sorted = sorted_init.shape[0]
    mt = MOVE_TILE
    return pl.pallas_call(
        _dispatch_kernel,
        grid=(n // mt,),
        in_specs=[
            pl.BlockSpec((1, 1, mt), lambda i: (i, 0, 0), memory_space=pltpu.SMEM),
            pl.BlockSpec((mt, width), lambda i: (i, 0)),
            pl.BlockSpec(memory_space=pl.ANY),
        ],
        out_specs=pl.BlockSpec(memory_space=pl.ANY),
        out_shape=jax.ShapeDtypeStruct((n_sorted, width), F32),
        scratch_shapes=[pltpu.SemaphoreType.DMA(())],
        input_output_aliases={2: 0},
        compiler_params=_params(1),
        name="moe_dispatch",
    )(pos.reshape(n // mt, 1, mt), rows_tok, sorted_init)


def _expert_kernel(lo_ref, hi_ref, live_ref, hs_ref, wgu_lo, wgu_hi, wd_lo, wd_hi, y_ref):
    del lo_ref, hi_ref
    t = pl.program_id(0)

    @pl.when(live_ref[t] != 0)
    def _():
        hs = hs_ref[...]
        h = hs[:, :D_MODEL].astype(BF16)

        def expert(wgu_ref, wd_ref, w):
            gu = jnp.dot(h, wgu_ref[0], preferred_element_type=F32)
            hid = _silu(gu[:, :D_FF_EXPERT]) * gu[:, D_FF_EXPERT:] * w
            return jnp.dot(hid.astype(BF16), wd_ref[0], preferred_element_type=F32)

        y = (expert(wgu_lo, wd_lo, hs[:, D_MODEL:D_MODEL + 1])
             + expert(wgu_hi, wd_hi, hs[:, D_MODEL + 1:D_MODEL + 2]))
        for c in range(LANE_GROUPS):
            y_ref[pl.ds(c, y.shape[0], stride=LANE_GROUPS), :] = y[:, c * LANES:(c + 1) * LANES]

    @pl.when(live_ref[t] == 0)
    def _():
        y_ref[...] = jnp.zeros_like(y_ref)


def _experts(tile_lo, tile_hi, tile_live, rows_sorted, w_gu, w_down):
    n_sorted = rows_sorted.shape[0]
    te = EXPERT_TILE
    grid_spec = pltpu.PrefetchScalarGridSpec(
        num_scalar_prefetch=3,
        grid=(n_sorted // te,),
        in_specs=[
            pl.BlockSpec((te, ROW_WORDS), lambda t, lo, hi, live: (t, 0)),
            pl.BlockSpec((1, D_MODEL, 2 * D_FF_EXPERT), lambda t, lo, hi, live: (lo[t], 0, 0)),
            pl.BlockSpec((1, D_MODEL, 2 * D_FF_EXPERT), lambda t, lo, hi, live: (hi[t], 0, 0)),
            pl.BlockSpec((1, D_FF_EXPERT, D_MODEL), lambda t, lo, hi, live: (lo[t], 0, 0)),
            pl.BlockSpec((1, D_FF_EXPERT, D_MODEL), lambda t, lo, hi, live: (hi[t], 0, 0)),
        ],
        out_specs=pl.BlockSpec((te * LANE_GROUPS, LANES), lambda t, lo, hi, live: (t, 0)),
    )
    return pl.pallas_call(
        _expert_kernel,
        grid_spec=grid_spec,
        out_shape=jax.ShapeDtypeStruct((n_sorted * LANE_GROUPS, LANES), F32),
        compiler_params=_params(1),
        name="moe_experts",
    )(tile_lo, tile_hi, tile_live, rows_sorted, w_gu, w_gu, w_down, w_down)


def _combine_kernel(pos_ref, x1_ref, mod_ref, ys_ref, o_ref, buf, sem):
    rows = pos_ref.shape[-1]
    g = LANE_GROUPS

    def row_copy(base, j):
        src = pl.multiple_of(pos_ref[0, 0, base + j] * g, g)
        return pltpu.make_async_copy(ys_ref.at[pl.ds(src, g)], buf.at[pl.ds((base + j) * g, g)], sem)

    _move_rows(rows, row_copy)
    y = jnp.concatenate([buf[pl.ds(c, rows, stride=g), :] for c in range(g)], axis=1)
    o_ref[0] = x1_ref[0] + mod_ref[0][5:6] * y


def _combine(pos, x1, mod_l, y_sorted):
    bsz, seq, d = x1.shape
    mt = MOVE_TILE
    per_seq = seq // mt
    return pl.pallas_call(
        _combine_kernel,
        grid=(bsz, per_seq),
        in_specs=[
            pl.BlockSpec((1, 1, mt), lambda b, t: (b * per_seq + t, 0, 0), memory_space=pltpu.SMEM),
            pl.BlockSpec((1, mt, d), lambda b, t: (b, t, 0)),
            pl.BlockSpec((1, 6, d), lambda b, t: (b, 0, 0)),
            pl.BlockSpec(memory_space=pl.ANY),
        ],
        out_specs=pl.BlockSpec((1, mt, d), lambda b, t: (b, t, 0)),
        out_shape=jax.ShapeDtypeStruct((bsz, seq, d), F32),
        scratch_shapes=[pltpu.VMEM((mt * LANE_GROUPS, LANES), F32), pltpu.SemaphoreType.DMA(())],
        compiler_params=_params(2),
        name="moe_combine",
    )(pos.reshape(bsz * per_seq, 1, mt), x1, mod_l, y_sorted)


def _routing_tables(meta, counts, n_tiles):
    te = EXPERT_TILE
    bucket = meta[:, 0].astype(jnp.int32)
    rank = meta[:, 1].astype(jnp.int32)
    counts = counts[0, :N_BUCKETS].astype(jnp.int32)
    tiles_per = (counts + te - 1) // te
    tile_end = jnp.cumsum(tiles_per)
    offsets = (tile_end - tiles_per) * te
    pos = offsets[bucket] + rank
    tile_ids = jnp.arange(n_tiles, dtype=jnp.int32)
    live = tile_ids < tile_end[-1]
    last_live = jnp.maximum(tile_end[-1] - 1, 0)
    tile_bucket = jnp.sum(jnp.minimum(tile_ids, last_live)[:, None] >= tile_end[None, :], axis=1, dtype=jnp.int32)
    tile_bucket = jnp.minimum(tile_bucket, N_BUCKETS - 1)
    pair_lo = jnp.array([0, 0, 0, 1, 1, 2], jnp.int32)
    pair_hi = jnp.array([1, 2, 3, 2, 3, 3], jnp.int32)
    group0 = (tile_bucket // PAIRS_PER_GROUP) * EXPERTS_PER_GROUP
    pair = tile_bucket % PAIRS_PER_GROUP
    return pos, group0 + pair_lo[pair], group0 + pair_hi[pair], live.astype(jnp.int32)


def kernel(x, c, positions, norm1_g, norm2_g, w_ada, b_ada, w_in, conv_w, q_norm_g, k_norm_g, conv_out_g,
           attn_out_g, w_out, w_router_group, b_router_group, w_router_expert, b_router_expert,
           w_gate, w_up, w_down):
    bsz, seq, d = x.shape
    depth = w_in.shape[0]
    n = bsz * seq
    tm = TOKEN_TILE
    n_tiles = n // EXPERT_TILE + N_BUCKETS
    n_sorted = n_tiles * EXPERT_TILE

    mod = _modulation(c, w_ada, b_ada).reshape(depth, bsz, 6, d)
    cos_t, sin_t = _rope_tables(positions)

    w_in_b = w_in.astype(BF16)
    w_out_b = w_out.astype(BF16)
    w_gu_b = jnp.concatenate([w_gate, w_up], axis=-1).astype(BF16)
    w_down_b = w_down.astype(BF16)
    pad = LANES - N_GROUPS - N_EXPERTS
    w_router = jnp.concatenate(
        [w_router_group, w_router_expert, jnp.zeros((depth, d, pad), F32)], axis=-1).astype(BF16)
    b_router = jnp.concatenate(
        [b_router_group, b_router_expert, jnp.zeros((depth, pad), F32)], axis=-1).reshape(depth, 1, LANES)
    q_gain = jnp.broadcast_to(q_norm_g[:, :, None], (depth, HEAD_DIM, tm))
    k_gain = jnp.broadcast_to(k_norm_g[:, :, None], (depth, HEAD_DIM, tm))
    idx = jnp.arange(tm)
    tri = (idx[None, :] < idx[:, None]).astype(BF16)

    rows_sorted = jnp.zeros((n_sorted, ROW_WORDS), F32)
    pending = None
    for l in range(depth):
        layer_args = (mod[l], norm1_g[l][None], w_in_b[l], conv_w[l], conv_out_g[l][None],
                      q_gain[l], k_gain[l], cos_t, sin_t)
        if pending is None:
            yc, q_t, k, v_t, kmean = _mixer_in(x, layer_args)
        else:
            yc, q_t, k, v_t, kmean, x = _mixer_in(x1, layer_args, combine=pending)
        ya = _attention(q_t, k, v_t, kmean.reshape(bsz, seq // tm, ATT_WIDTH))
        x1, rows_tok, meta, counts = _mixer_out(
            x, yc, ya, mod[l], attn_out_g[l][None], w_out_b[l], norm2_g[l][None],
            w_router[l], b_router[l], tri)
        pos, tile_lo, tile_hi, tile_live = _routing_tables(meta.reshape(n, LANES), counts, n_tiles)
        rows_sorted = _dispatch(pos, rows_tok.reshape(n, ROW_WORDS), rows_sorted)
        y_sorted = _experts(tile_lo, tile_hi, tile_live, rows_sorted, w_gu_b[l], w_down_b[l])
        pending = (pos, mod[l], y_sorted)
    return _combine(pending[0], x1, pending[1], pending[2])
```

```python
import functools

import jax
import jax.numpy as jnp
from jax import lax
from jax.experimental import pallas as pl
from jax.experimental.pallas import tpu as pltpu

F32 = jnp.float32
BF16 = jnp.bfloat16

D_MODEL = 1024
CONV_WIDTH = 512
CONV_K = 3
HEAD_DIM = 64
HALF_DIM = HEAD_DIM // 2
ATT_WIDTH = 512
ATT_HEADS = ATT_WIDTH // HEAD_DIM
IN_WIDTH = 3 * CONV_WIDTH + 3 * ATT_WIDTH
MOBA_BLOCK = 256
MOBA_TOPK = 3
ROPE_THETA = 10000.0
N_GROUPS = 4
EXPERTS_PER_GROUP = 4
N_EXPERTS = N_GROUPS * EXPERTS_PER_GROUP
D_FF_EXPERT = D_MODEL // 4
EPS = 1e-6
NEG = -1e30
LOWEST = -3e38
LOG2_E = 1.4426950408889634
Q_SCALE = HEAD_DIM ** -0.5 * LOG2_E

LANES = 128
LANE_GROUPS = D_MODEL // LANES
HEAD_PAIR = 2 * HEAD_DIM
N_PAIRS = ATT_WIDTH // HEAD_PAIR
VT_ROWS = HEAD_PAIR + 16
TOKEN_TILE = MOBA_BLOCK
PAIRS_PER_GROUP = 6
N_BUCKETS = N_GROUPS * PAIRS_PER_GROUP
EXPERT_TILE = 256
ROW_WORDS = D_MODEL + LANES
MOVE_TILE = 512
ROW_UNROLL = 8
VMEM_LIMIT = 48 * 1024 * 1024


def _params(n_axes, vmem=VMEM_LIMIT):
    return pltpu.CompilerParams(dimension_semantics=("arbitrary",) * n_axes, vmem_limit_bytes=vmem)


def _rmsnorm(x, g):
    return x * lax.rsqrt(jnp.mean(x * x, axis=-1, keepdims=True) + EPS) * g


def _silu(x):
    return x * (1.0 / (1.0 + jnp.exp(-x)))


def _for_rows(rows, row_copy, method):
    def body(g, carry):
        base = pl.multiple_of(g * ROW_UNROLL, ROW_UNROLL)
        for j in range(ROW_UNROLL):
            getattr(row_copy(base, j), method)()
        return carry

    lax.fori_loop(0, rows // ROW_UNROLL, body, 0)


def _mod_kernel(c_ref, w_ref, b_ref, o_ref):
    ca = _silu(c_ref[...]).astype(BF16)
    o_ref[0] = jnp.dot(ca, w_ref[0].astype(BF16), preferred_element_type=F32) + b_ref[0]


def _modulation(c, w_ada, b_ada):
    depth, d, width = w_ada.shape
    bsz = c.shape[0]
    tn = 1536
    return pl.pallas_call(
        _mod_kernel,
        grid=(depth, width // tn),
        in_specs=[
            pl.BlockSpec((bsz, d), lambda l, j: (0, 0)),
            pl.BlockSpec((1, d, tn), lambda l, j: (l, 0, j)),
            pl.BlockSpec((1, 1, tn), lambda l, j: (l, 0, j)),
        ],
        out_specs=pl.BlockSpec((1, bsz, tn), lambda l, j: (l, 0, j)),
        out_shape=jax.ShapeDtypeStruct((depth, bsz, width), F32),
        compiler_params=_params(2),
        name="modulation",
    )(c, w_ada, b_ada.reshape(depth, 1, width))


def _rope_kernel(pos_ref, freq_ref, cos_ref, sin_ref):
    ts = pos_ref.shape[-1]
    freq = jnp.concatenate([freq_ref[...]] * (ts // LANES), axis=1)
    ang = pos_ref[0] * freq
    cos_ref[0] = jnp.cos(ang)
    sin_ref[0] = jnp.sin(ang)


def _rope_tables(positions):
    bsz, seq = positions.shape
    ts = 512
    inv_freq = ROPE_THETA ** (-jnp.arange(0, HEAD_DIM, 2, dtype=F32) / HEAD_DIM)
    freq = jnp.broadcast_to(inv_freq[:, None], (HALF_DIM, LANES))
    pos = positions.astype(F32).reshape(bsz, 1, seq)
    spec = pl.BlockSpec((1, HALF_DIM, ts), lambda b, t: (b, 0, t))
    return pl.pallas_call(
        _rope_kernel,
        grid=(bsz, seq // ts),
        in_specs=[pl.BlockSpec((1, 1, ts), lambda b, t: (b, 0, t)),
                  pl.BlockSpec((HALF_DIM, LANES), lambda b, t: (0, 0))],
        out_specs=[spec, spec],
        out_shape=[jax.ShapeDtypeStruct((bsz, HALF_DIM, seq), F32)] * 2,
        compiler_params=_params(2),
        name="rope_tables",
    )(pos, freq)


def _headnorm_rope_t(z_t, gain, cos_t, sin_t):
    outs = []
    for hd in range(ATT_HEADS):
        z = z_t[hd * HEAD_DIM:(hd + 1) * HEAD_DIM]
        zn = z * lax.rsqrt(jnp.mean(z * z, axis=0, keepdims=True) + EPS) * gain
        x1, x2 = zn[:HALF_DIM], zn[HALF_DIM:]
        outs.append(x1 * cos_t - x2 * sin_t)
        outs.append(x2 * cos_t + x1 * sin_t)
    return jnp.concatenate(outs, axis=0)


def _mixer_in_kernel(x_ref, *refs):
    _mixer_in_body(x_ref[0], *refs)


def _mixer_in_fused_kernel(pos_ref, pos_next_ref, x1_ref, mod_prev_ref, ys_ref, *refs):
    (*body_refs, x_ref, carry_ref, land0, land1, y_ref, sems) = refs
    tm = x1_ref.shape[1]
    g = LANE_GROUPS
    step = pl.program_id(0) * pl.num_programs(1) + pl.program_id(1)
    last = pl.num_programs(0) * pl.num_programs(1) - 1
    landing = (land0, land1)

    def gather(idx_ref, slot, method):
        def row_copy(base, j):
            src = pl.multiple_of(idx_ref[0, 0, base + j] * g, g)
            return pltpu.make_async_copy(ys_ref.at[pl.ds(src, g)], landing[slot].at[pl.ds((base + j) * g, g)],
                                         sems.at[slot])
        _for_rows(tm, row_copy, method)

    pl.when(step == 0)(lambda: gather(pos_ref, 0, "start"))
    for slot in range(2):
        pl.when((step < last) & ((step + 1) % 2 == slot))(functools.partial(gather, pos_next_ref, slot, "start"))
    for slot in range(2):
        @pl.when(step % 2 == slot)
        def _(slot=slot):
            gather(pos_ref, slot, "wait")
            y_ref[...] = jnp.concatenate([landing[slot][pl.ds(c, tm, stride=g), :] for c in range(g)], axis=1)

    x = x1_ref[0] + mod_prev_ref[0][5:6] * y_ref[...]
    x_ref[0] = x
    _mixer_in_body(x, *body_refs, carry_ref)


def _mixer_in_body(x, mod_ref, g1_ref, win_ref, cw_ref, cg_ref, qg_ref, kg_ref, cos_ref, sin_ref,
                   yc_ref, qt_ref, k_ref, vt_ref, km_ref, carry_ref):
    tm = x.shape[0]

    @pl.when(pl.program_id(1) == 0)
    def _():
        carry_ref[...] = jnp.zeros_like(carry_ref)

    mod = mod_ref[0]
    h = _rmsnorm(x, g1_ref[...]) * (1.0 + mod[1:2]) + mod[0:1]
    u = jnp.dot(h.astype(BF16), win_ref[...], preferred_element_type=F32)

    cw = CONV_WIDTH
    u_b, gated = u[:, :cw], u[:, cw:2 * cw] * u[:, 2 * cw:3 * cw]
    row = lax.broadcasted_iota(jnp.int32, (tm, 1), 0)
    tail = carry_ref[...]
    prev1 = jnp.where(row == 0, tail[7:8], pltpu.roll(gated, 1, 0))
    prev2 = jnp.where(row == 0, tail[6:7], jnp.where(row == 1, tail[7:8], pltpu.roll(gated, 2, 0)))
    taps = cw_ref[...]
    y_conv = u_b * (taps[0:1] * prev2 + taps[1:2] * prev1 + taps[2:3] * gated)
    carry_ref[...] = gated[tm - 8:]
    yc_ref[0] = _rmsnorm(y_conv, cg_ref[...]).astype(BF16)

    a0 = 3 * cw
    cos_t, sin_t = cos_ref[0], sin_ref[0]
    q_t = _headnorm_rope_t(u[:, a0:a0 + ATT_WIDTH].T, qg_ref[...], cos_t, sin_t)
    k_t = _headnorm_rope_t(u[:, a0 + ATT_WIDTH:a0 + 2 * ATT_WIDTH].T, kg_ref[...], cos_t, sin_t)
    k = k_t.T
    qt_ref[0] = (q_t * Q_SCALE).astype(BF16)
    k_ref[0] = k.astype(BF16)
    km_ref[0, 0] = jnp.mean(k, axis=0, keepdims=True)
    v_t = u[:, a0 + 2 * ATT_WIDTH:].T
    ones = jnp.ones((VT_ROWS - HEAD_PAIR, tm), F32)
    vt_ref[0] = jnp.concatenate(
        [piece for p in range(N_PAIRS) for piece in (v_t[p * HEAD_PAIR:(p + 1) * HEAD_PAIR], ones)],
        axis=0).astype(BF16)


def _mixer_in(x, layer_args, combine=None):
    bsz, seq, d = x.shape
    tm = TOKEN_TILE
    nb = seq // tm
    const2 = lambda b, t: (0, 0)
    tok = pl.BlockSpec((1, tm, d), lambda b, t: (b, t, 0))
    per_batch = pl.BlockSpec((1, 6, d), lambda b, t: (b, 0, 0))
    transposed = pl.BlockSpec((1, ATT_WIDTH, tm), lambda b, t: (b, 0, t))
    in_specs = [
        per_batch,
        pl.BlockSpec((1, d), const2),
        pl.BlockSpec((d, IN_WIDTH), const2),
        pl.BlockSpec((CONV_K, CONV_WIDTH), const2),
        pl.BlockSpec((1, CONV_WIDTH), const2),
        pl.BlockSpec((HEAD_DIM, tm), const2),
        pl.BlockSpec((HEAD_DIM, tm), const2),
        pl.BlockSpec((1, HALF_DIM, tm), lambda b, t: (b, 0, t)),
        pl.BlockSpec((1, HALF_DIM, tm), lambda b, t: (b, 0, t)),
    ]
    out_specs = [
        pl.BlockSpec((1, tm, CONV_WIDTH), lambda b, t: (b, t, 0)),
        transposed,
        pl.BlockSpec((1, tm, ATT_WIDTH), lambda b, t: (b, t, 0)),
        pl.BlockSpec((1, N_PAIRS * VT_ROWS, tm), lambda b, t: (b, 0, t)),
        pl.BlockSpec((1, 1, 1, ATT_WIDTH), lambda b, t: (b, t, 0, 0)),
    ]
    out_shape = [
        jax.ShapeDtypeStruct((bsz, seq, CONV_WIDTH), BF16),
        jax.ShapeDtypeStruct((bsz, ATT_WIDTH, seq), BF16),
        jax.ShapeDtypeStruct((bsz, seq, ATT_WIDTH), BF16),
        jax.ShapeDtypeStruct((bsz, N_PAIRS * VT_ROWS, seq), BF16),
        jax.ShapeDtypeStruct((bsz, nb, 1, ATT_WIDTH), F32),
    ]
    scratch = [pltpu.VMEM((8, CONV_WIDTH), F32)]
    if combine is None:
        return pl.pallas_call(
            _mixer_in_kernel, grid=(bsz, nb), in_specs=[tok] + in_specs, out_specs=out_specs,
            out_shape=out_shape, scratch_shapes=scratch, compiler_params=_params(2), name="mixer_in",
        )(x, *layer_args)

    pos, mod_prev, y_sorted = combine
    n_blocks = bsz * nb
    pos_blocks = pos.reshape(n_blocks, 1, tm)
    landing = pltpu.VMEM((tm * LANE_GROUPS, LANES), F32)
    return pl.pallas_call(
        _mixer_in_fused_kernel,
        grid=(bsz, nb),
        in_specs=[
            pl.BlockSpec((1, 1, tm), lambda b, t: (b * nb + t, 0, 0), memory_space=pltpu.SMEM),
            pl.BlockSpec((1, 1, tm), lambda b, t: (jnp.minimum(b * nb + t + 1, n_blocks - 1), 0, 0),
                         memory_space=pltpu.SMEM),
            tok, per_batch, pl.BlockSpec(memory_space=pl.ANY),
        ] + in_specs,
        out_specs=out_specs + [tok],
        out_shape=out_shape + [jax.ShapeDtypeStruct((bsz, seq, d), F32)],
        scratch_shapes=scratch + [landing, landing, pltpu.VMEM((tm, d), F32), pltpu.SemaphoreType.DMA((2,))],
        compiler_params=_params(2),
        name="mixer_in_combine",
    )(pos_blocks, pos_blocks, x, mod_prev, y_sorted, *layer_args)


def _block_bias(q_t, km, own, n_sel):
    tq = q_t.shape[-1]
    if own <= n_sel:
        return None
    km = km[:own]
    km_hi = km.astype(BF16).astype(F32)
    km_lo = km - km_hi
    first = lax.broadcasted_iota(jnp.int32, (1, HEAD_PAIR), 1) < HEAD_DIM
    km_rows = jnp.concatenate([jnp.where(first, km_hi, 0.0), jnp.where(first, 0.0, km_hi),
                               jnp.where(first, km_lo, 0.0), jnp.where(first, 0.0, km_lo)], axis=0)
    gate = jnp.dot(km_rows.astype(BF16), q_t, preferred_element_type=F32)
    blk = lax.broadcasted_iota(jnp.int32, (own, tq), 0)
    biases = []
    for h in range(2):
        g = gate[h * own:(h + 1) * own] + gate[(2 + h) * own:(3 + h) * own]
        rank = jnp.zeros((own, tq), F32)
        for m in range(own):
            gm = g[m:m + 1]
            ahead = (gm > g) | ((gm == g) & (blk > m))
            rank = rank + jnp.where(ahead, 1.0, 0.0)
        biases.append(jnp.where(rank < n_sel, 0.0, NEG))
    return biases


def _attend_scores(own, half, n_sel, tq, qt_ref, k_ref, km_ref):
    hq = tq // 2
    q0 = own * tq + half * hq
    q_t = qt_ref[0, :, q0:q0 + hq]
    dead = jnp.zeros((HEAD_DIM, hq), BF16)
    q_wide = jnp.concatenate([jnp.concatenate([q_t[:HEAD_DIM], dead], axis=0),
                              jnp.concatenate([dead, q_t[HEAD_DIM:]], axis=0)], axis=1)
    biases = _block_bias(q_t, km_ref[0], own, n_sel)
    if biases is not None:
        biases = jnp.concatenate(biases, axis=1)
    causal = (lax.broadcasted_iota(jnp.int32, (tq, hq), 0)
              <= lax.broadcasted_iota(jnp.int32, (tq, hq), 1) + half * hq)
    causal = jnp.concatenate([causal, causal], axis=1)
    m = None
    scores = []
    for n in range(own + 1):
        s = jnp.dot(k_ref[0, n * tq:(n + 1) * tq, :], q_wide, preferred_element_type=F32)
        bias = None
        if n == own:
            s = jnp.where(causal, s, NEG)
        elif biases is not None:
            bias = biases[n:n + 1]
        scores.append((s, bias))
        blk_max = jnp.max(s, axis=0, keepdims=True)
        if bias is not None:
            blk_max = blk_max + bias
        m = blk_max if m is None else jnp.maximum(m, blk_max)
    return scores, m


def _attend_finish(own, half, tq, scores, m, vt_ref, o_ref):
    hq = tq // 2
    q0 = own * tq + half * hq
    n_keys = (own + 1) * tq
    probs = []
    for s, bias in scores:
        shift = m if bias is None else m - bias
        probs.append(jnp.exp2(s - shift).astype(BF16))
    acc = jnp.dot(vt_ref[0, :, :n_keys], jnp.concatenate(probs, axis=0), preferred_element_type=F32)
    acc = acc[:HEAD_PAIR] * (1.0 / acc[HEAD_PAIR:HEAD_PAIR + 1])
    o_t = jnp.concatenate([acc[:HEAD_DIM, :hq], acc[HEAD_DIM:, hq:]], axis=0)
    o_ref[0, q0:q0 + hq, :] = o_t.T.astype(BF16)


def _attn_kernel(qt_ref, k_ref, vt_ref, km_ref, o_ref):
    tq = MOBA_BLOCK
    nb = k_ref.shape[1] // tq
    n_sel = max(1, min(MOBA_TOPK, nb - 1))
    steps = [(own, half) for own in range(nb) for half in range(2)]
    ahead = _attend_scores(*steps[0], n_sel, tq, qt_ref, k_ref, km_ref)
    for i, (own, half) in enumerate(steps):
        scores, m = ahead
        if i + 1 < len(steps):
            ahead = _attend_scores(*steps[i + 1], n_sel, tq, qt_ref, k_ref, km_ref)
        _attend_finish(own, half, tq, scores, m, vt_ref, o_ref)


def _attention(q_t, k, v_t, kmean):
    bsz, seq, _ = k.shape
    tq = MOBA_BLOCK
    nb = seq // tq
    return pl.pallas_call(
        _attn_kernel,
        grid=(bsz, N_PAIRS),
        in_specs=[
            pl.BlockSpec((1, HEAD_PAIR, seq), lambda b, p: (b, p, 0)),
            pl.BlockSpec((1, seq, HEAD_PAIR), lambda b, p: (b, 0, p)),
            pl.BlockSpec((1, VT_ROWS, seq), lambda b, p: (b, p, 0)),
            pl.BlockSpec((1, nb, HEAD_PAIR), lambda b, p: (b, 0, p)),
        ],
        out_specs=pl.BlockSpec((1, seq, HEAD_PAIR), lambda b, p: (b, 0, p)),
        out_shape=jax.ShapeDtypeStruct((bsz, seq, ATT_WIDTH), BF16),
        compiler_params=_params(2),
        name="moba_attention",
    )(q_t, k, v_t, kmean)


def _mixer_out_kernel(x_ref, yc_ref, ya_ref, mod_ref, ag_ref, wout_ref, g2_ref, wr_ref, br_ref, tri_ref,
                      x1_ref, hs_ref, meta_ref, cnt_ref):
    @pl.when((pl.program_id(0) == 0) & (pl.program_id(1) == 0))
    def _():
        cnt_ref[...] = jnp.zeros_like(cnt_ref)

    mod = mod_ref[0]
    ya = _rmsnorm(ya_ref[0].astype(F32), ag_ref[...]).astype(BF16)
    proj = (jnp.dot(yc_ref[0], wout_ref[:CONV_WIDTH], preferred_element_type=F32)
            + jnp.dot(ya, wout_ref[CONV_WIDTH:], preferred_element_type=F32))
    x1 = x_ref[0] + mod[2:3] * proj
    x1_ref[0] = x1
    h2 = _rmsnorm(x1, g2_ref[...]) * (1.0 + mod[4:5]) + mod[3:4]

    logits = jnp.dot(h2.astype(BF16), wr_ref[...], preferred_element_type=F32) + br_ref[...]
    lane = lax.broadcasted_iota(jnp.int32, (1, LANES), 1).astype(F32)
    is_group = lane < N_GROUPS
    gl = jnp.where(is_group, logits, LOWEST)
    g_max = jnp.max(gl, axis=-1, keepdims=True)
    g_sel = jnp.min(jnp.where(gl == g_max, lane, float(LANES)), axis=-1, keepdims=True)
    p_group = 1.0 / jnp.sum(jnp.where(is_group, jnp.exp(gl - g_max), 0.0), axis=-1, keepdims=True)

    e0 = N_GROUPS + EXPERTS_PER_GROUP * g_sel
    el = jnp.where((lane >= e0) & (lane < e0 + EXPERTS_PER_GROUP), logits, LOWEST)
    v1 = jnp.max(el, axis=-1, keepdims=True)
    i1 = jnp.min(jnp.where(el == v1, lane, float(LANES)), axis=-1, keepdims=True)
    el2 = jnp.where(lane == i1, LOWEST, el)
    v2 = jnp.max(el2, axis=-1, keepdims=True)
    i2 = jnp.min(jnp.where(el2 == v2, lane, float(LANES)), axis=-1, keepdims=True)
    ex = jnp.exp(v2 - v1)
    w_top1 = p_group / (1.0 + ex)
    w_top2 = w_top1 * ex

    a, b = i1 - e0, i2 - e0
    lo, hi = jnp.minimum(a, b), jnp.maximum(a, b)
    bucket = g_sel * PAIRS_PER_GROUP + lo * (7.0 - lo) * 0.5 + (hi - lo - 1.0)
    w_lo = jnp.where(a < b, w_top1, w_top2)
    w_hi = jnp.where(a < b, w_top2, w_top1)

    onehot = jnp.where(lane == bucket, 1.0, 0.0)
    before = jnp.dot(tri_ref[...], onehot.astype(BF16), preferred_element_type=F32)
    cnt = cnt_ref[...]
    rank = jnp.sum((before + cnt) * onehot, axis=-1, keepdims=True)
    cnt_ref[...] = cnt + jnp.sum(onehot, axis=0, keepdims=True)

    meta_ref[0] = jnp.where(lane == 0.0, bucket, jnp.where(lane == 1.0, rank, 0.0))
    hs_ref[0, :, :D_MODEL] = h2
    hs_ref[0, :, D_MODEL:] = jnp.where(lane == 0.0, w_lo, jnp.where(lane == 1.0, w_hi, 0.0))


def _mixer_out(x, yc, ya, mod_l, attn_g, w_out, g2, w_router, b_router, tri):
    bsz, seq, d = x.shape
    tm = TOKEN_TILE
    const2 = lambda b, t: (0, 0)
    tok = lambda width: pl.BlockSpec((1, tm, width), lambda b, t: (b, t, 0))
    return pl.pallas_call(
        _mixer_out_kernel,
        grid=(bsz, seq // tm),
        in_specs=[
            tok(d), tok(CONV_WIDTH), tok(ATT_WIDTH),
            pl.BlockSpec((1, 6, d), lambda b, t: (b, 0, 0)),
            pl.BlockSpec((1, ATT_WIDTH), const2),
            pl.BlockSpec((d, d), const2),
            pl.BlockSpec((1, d), const2),
            pl.BlockSpec((d, LANES), const2),
            pl.BlockSpec((1, LANES), const2),
            pl.BlockSpec((tm, tm), const2),
        ],
        out_specs=[tok(d), tok(ROW_WORDS), tok(LANES), pl.BlockSpec((1, LANES), const2)],
        out_shape=[
            jax.ShapeDtypeStruct((bsz, seq, d), F32),
            jax.ShapeDtypeStruct((bsz, seq, ROW_WORDS), F32),
            jax.ShapeDtypeStruct((bsz, seq, LANES), F32),
            jax.ShapeDtypeStruct((1, LANES), F32),
        ],
        compiler_params=_params(2),
        name="mixer_out",
    )(x, yc, ya, mod_l, attn_g, w_out, g2, w_router, b_router, tri)


def _move_rows(rows, row_copy):
    _for_rows(rows, row_copy, "start")
    _for_rows(rows, row_copy, "wait")


def _dispatch_kernel(pos_ref, src_ref, dst_in_ref, dst_ref, sem):
    del dst_in_ref

    def row_copy(base, j):
        return pltpu.make_async_copy(src_ref.at[pl.ds(base + j, 1)],
                                     dst_ref.at[pl.ds(pos_ref[0, 0, base + j], 1)], sem)

    _move_rows(pos_ref.shape[-1], row_copy)


def _dispatch(pos, rows_tok, sorted_init):
    n, width = rows_tok.shape
    n_sorted = sorted_init.shape[0]
    mt = MOVE_TILE
    return pl.pallas_call(
        _dispatch_kernel,
        grid=(n // mt,),
        in_specs=[
            pl.BlockSpec((1, 1, mt), lambda i: (i, 0, 0), memory_space=pltpu.SMEM),
            pl.BlockSpec((mt, width), lambda i: (i, 0)),
            pl.BlockSpec(memory_space=pl.ANY),
        ],
        out_specs=pl.BlockSpec(memory_space=pl.ANY),
        out_shape=jax.ShapeDtypeStruct((n_sorted, width), F32),
        scratch_shapes=[pltpu.SemaphoreType.DMA(())],
        input_output_aliases={2: 0},
        compiler_params=_params(1),
        name="moe_dispatch",
    )(pos.reshape(n // mt, 1, mt), rows_tok, sorted_init)


def _expert_kernel(lo_ref, hi_ref, live_ref, hs_ref, wgu_lo, wgu_hi, wd_lo, wd_hi, y_ref):
    del lo_ref, hi_ref
    t = pl.program_id(0)

    @pl.when(live_ref[t] != 0)
    def _():
        hs = hs_ref[...]
        h = hs[:, :D_MODEL].astype(BF16)

        def expert(wgu_ref, wd_ref, w):
            gu = jnp.dot(h, wgu_ref[0], preferred_element_type=F32)
            hid = _silu(gu[:, :D_FF_EXPERT]) * gu[:, D_FF_EXPERT:] * w
            return jnp.dot(hid.astype(BF16), wd_ref[0], preferred_element_type=F32)

        y = (expert(wgu_lo, wd_lo, hs[:, D_MODEL:D_MODEL + 1])
             + expert(wgu_hi, wd_hi, hs[:, D_MODEL + 1:D_MODEL + 2]))
        for c in range(LANE_GROUPS):
            y_ref[pl.ds(c, y.shape[0], stride=LANE_GROUPS), :] = y[:, c * LANES:(c + 1) * LANES]

    @pl.when(live_ref[t] == 0)
    def _():
        y_ref[...] = jnp.zeros_like(y_ref)


def _experts(tile_lo, tile_hi, tile_live, rows_sorted, w_gu, w_down):
    n_sorted = rows_sorted.shape[0]
    te = EXPERT_TILE
    grid_spec = pltpu.PrefetchScalarGridSpec(
        num_scalar_prefetch=3,
        grid=(n_sorted // te,),
        in_specs=[
            pl.BlockSpec((te, ROW_WORDS), lambda t, lo, hi, live: (t, 0)),
            pl.BlockSpec((1, D_MODEL, 2 * D_FF_EXPERT), lambda t, lo, hi, live: (lo[t], 0, 0)),
            pl.BlockSpec((1, D_MODEL, 2 * D_FF_EXPERT), lambda t, lo, hi, live: (hi[t], 0, 0)),
            pl.BlockSpec((1, D_FF_EXPERT, D_MODEL), lambda t, lo, hi, live: (lo[t], 0, 0)),
            pl.BlockSpec((1, D_FF_EXPERT, D_MODEL), lambda t, lo, hi, live: (hi[t], 0, 0)),
        ],
        out_specs=pl.BlockSpec((te * LANE_GROUPS, LANES), lambda t, lo, hi, live: (t, 0)),
    )
    return pl.pallas_call(
        _expert_kernel,
        grid_spec=grid_spec,
        out_shape=jax.ShapeDtypeStruct((n_sorted * LANE_GROUPS, LANES), F32),
        compiler_params=_params(1),
        name="moe_experts",
    )(tile_lo, tile_hi, tile_live, rows_sorted, w_gu, w_gu, w_down, w_down)


def _combine_kernel(pos_ref, x1_ref, mod_ref, ys_ref, o_ref, buf, sem):
    rows = pos_ref.shape[-1]
    g = LANE_GROUPS

    def row_copy(base, j):
        src = pl.multiple_of(pos_ref[0, 0, base + j] * g, g)
        return pltpu.make_async_copy(ys_ref.at[pl.ds(src, g)], buf.at[pl.ds((base + j) * g, g)], sem)

    _move_rows(rows, row_copy)
    y = jnp.concatenate([buf[pl.ds(c, rows, stride=g), :] for c in range(g)], axis=1)
    o_ref[0] = x1_ref[0] + mod_ref[0][5:6] * y


def _combine(pos, x1, mod_l, y_sorted):
    bsz, seq, d = x1.shape
    mt = MOVE_TILE
    per_seq = seq // mt
    return pl.pallas_call(
        _combine_kernel,
        grid=(bsz, per_seq),
        in_specs=[
            pl.BlockSpec((1, 1, mt), lambda b, t: (b * per_seq + t, 0, 0), memory_space=pltpu.SMEM),
            pl.BlockSpec((1, mt, d), lambda b, t: (b, t, 0)),
            pl.BlockSpec((1, 6, d), lambda b, t: (b, 0, 0)),
            pl.BlockSpec(memory_space=pl.ANY),
        ],
        out_specs=pl.BlockSpec((1, mt, d), lambda b, t: (b, t, 0)),
        out_shape=jax.ShapeDtypeStruct((bsz, seq, d), F32),
        scratch_shapes=[pltpu.VMEM((mt * LANE_GROUPS, LANES), F32), pltpu.SemaphoreType.DMA(())],
        compiler_params=_params(2),
        name="moe_combine",
    )(pos.reshape(bsz * per_seq, 1, mt), x1, mod_l, y_sorted)


def _routing_tables(meta, counts, n_tiles):
    te = EXPERT_TILE
    bucket = meta[:, 0].astype(jnp.int32)
    rank = meta[:, 1].astype(jnp.int32)
    counts = counts[0, :N_BUCKETS].astype(jnp.int32)
    tiles_per = (counts + te - 1) // te
    tile_end = jnp.cumsum(tiles_per)
    offsets = (tile_end - tiles_per) * te
    pos = offsets[bucket] + rank
    tile_ids = jnp.arange(n_tiles, dtype=jnp.int32)
    live = tile_ids < tile_end[-1]
    last_live = jnp.maximum(tile_end[-1] - 1, 0)
    tile_bucket = jnp.sum(jnp.minimum(tile_ids, last_live)[:, None] >= tile_end[None, :], axis=1, dtype=jnp.int32)
    tile_bucket = jnp.minimum(tile_bucket, N_BUCKETS - 1)
    pair_lo = jnp.array([0, 0, 0, 1, 1, 2], jnp.int32)
    pair_hi = jnp.array([1, 2, 3, 2, 3, 3], jnp.int32)
    group0 = (tile_bucket // PAIRS_PER_GROUP) * EXPERTS_PER_GROUP
    pair = tile_bucket % PAIRS_PER_GROUP
    return pos, group0 + pair_lo[pair], group0 + pair_hi[pair], live.astype(jnp.int32)


def kernel(x, c, positions, norm1_g, norm2_g, w_ada, b_ada, w_in, conv_w, q_norm_g, k_norm_g, conv_out_g,
           attn_out_g, w_out, w_router_group, b_router_group, w_router_expert, b_router_expert,
           w_gate, w_up, w_down):
    bsz, seq, d = x.shape
    depth = w_in.shape[0]
    n = bsz * seq
    tm = TOKEN_TILE
    n_tiles = n // EXPERT_TILE + N_BUCKETS
    n_sorted = n_tiles * EXPERT_TILE

    mod = _modulation(c, w_ada, b_ada).reshape(depth, bsz, 6, d)
    cos_t, sin_t = _rope_tables(positions)

    w_in_b = w_in.astype(BF16)
    w_out_b = w_out.astype(BF16)
    w_gu_b = jnp.concatenate([w_gate, w_up], axis=-1).astype(BF16)
    w_down_b = w_down.astype(BF16)
    pad = LANES - N_GROUPS - N_EXPERTS
    w_router = jnp.concatenate(
        [w_router_group, w_router_expert, jnp.zeros((depth, d, pad), F32)], axis=-1).astype(BF16)
    b_router = jnp.concatenate(
        [b_router_group, b_router_expert, jnp.zeros((depth, pad), F32)], axis=-1).reshape(depth, 1, LANES)
    q_gain = jnp.broadcast_to(q_norm_g[:, :, None], (depth, HEAD_DIM, tm))
    k_gain = jnp.broadcast_to(k_norm_g[:, :, None], (depth, HEAD_DIM, tm))
    idx = jnp.arange(tm)
    tri = (idx[None, :] < idx[:, None]).astype(BF16)

    rows_sorted = jnp.zeros((n_sorted, ROW_WORDS), F32)
    pending = None
    for l in range(depth):
        layer_args = (mod[l], norm1_g[l][None], w_in_b[l], conv_w[l], conv_out_g[l][None],
                      q_gain[l], k_gain[l], cos_t, sin_t)
        if pending is None:
            yc, q_t, k, v_t, kmean = _mixer_in(x, layer_args)
        else:
            yc, q_t, k, v_t, kmean, x = _mixer_in(x1, layer_args, combine=pending)
        ya = _attention(q_t, k, v_t, kmean.reshape(bsz, seq // tm, ATT_WIDTH))
        x1, rows_tok, meta, counts = _mixer_out(
            x, yc, ya, mod[l], attn_out_g[l][None], w_out_b[l], norm2_g[l][None],
            w_router[l], b_router[l], tri)
        pos, tile_lo, tile_hi, tile_live = _routing_tables(meta.reshape(n, LANES), counts, n_tiles)
        rows_sorted = _dispatch(pos, rows_tok.reshape(n, ROW_WORDS), rows_sorted)
        y_sorted = _experts(tile_lo, tile_hi, tile_live, rows_sorted, w_gu_b[l], w_down_b[l])
        pending = (pos, mod[l], y_sorted)
    return _combine(pending[0], x1, pending[1], pending[2])
```
